```python
import jax, jax.numpy as jnp
from jax import lax
import numpy as np

D_MODEL = 2048
BATCH = 4
SEQ = 8192
DEPTH = 1

HEAD_DIM = 64
N_HEADS_A = D_MODEL // (2 * HEAD_DIM)
N_KV_A = N_HEADS_A // 8
N_HEADS_B = D_MODEL // (2 * HEAD_DIM)
WINDOW_A = 128
DILATED_BRANCHES = ((128, 1), (512, 4), (2048, 16))
D_FF = 4 * D_MODEL
BLOCK = 128
EPS = 1e-5
NEG_INF = -1e30

Q_A = N_HEADS_A * HEAD_DIM
KV_A = N_KV_A * HEAD_DIM
Q_B = N_HEADS_B * HEAD_DIM
D_IN = Q_A + 2 * KV_A + 3 * Q_B
D_MIX = Q_A + Q_B

kernel_name = "hybrid_swa_sink_dilated_alibi_block"


def alibi_slopes(n):
    return jnp.asarray(2.0 ** (-8.0 * (np.arange(n) + 1) / n), dtype=jnp.float32)


def rmsnorm(x, g):
    x32 = x.astype(jnp.float32)
    y = x32 * lax.rsqrt(jnp.mean(x32 * x32, axis=-1, keepdims=True) + EPS)
    return y.astype(x.dtype) * g


def _with_prev_block(t, nb):
    b, L, G, Dh = t.shape
    tb = t.reshape(b, nb, BLOCK, G, Dh)
    prev = jnp.concatenate([jnp.zeros_like(tb[:, :1]), tb[:, :-1]], axis=1)
    return jnp.concatenate([prev, tb], axis=2)


def banded_attention(q, k, v, max_steps, step_dist, slopes, sinks):
    b, L, H, Dh = q.shape
    G = k.shape[2]
    R = H // G
    nb = L // BLOCK
    qb = q.reshape(b, nb, BLOCK, G, R, Dh)
    kb = _with_prev_block(k, nb)
    vb = _with_prev_block(v, nb)
    s = jnp.einsum('bnqgrd,bnkgd->bngrqk', qb, kb).astype(jnp.float32) * (Dh ** -0.5)
    qi = jnp.arange(BLOCK)[:, None]
    kj = jnp.arange(2 * BLOCK)[None, :]
    steps = qi + BLOCK - kj
    kpos = jnp.arange(nb)[:, None, None] * BLOCK + kj[None] - BLOCK
    valid = (steps >= 0) & (steps <= max_steps) & (kpos >= 0)
    alibi = slopes.reshape(G, R, 1, 1) * (step_dist * steps).astype(jnp.float32)
    s = jnp.where(valid[None, :, None, None], s - alibi[None, None], NEG_INF)
    m = jnp.max(s, axis=-1)
    if sinks is not None:
        sink = sinks.astype(jnp.float32).reshape(G, R, 1)
        m = jnp.maximum(m, sink)
    p = jnp.exp(s - m[..., None])
    denom = jnp.sum(p, axis=-1)
    if sinks is not None:
        denom = denom + jnp.exp(sink - m)
    o = jnp.einsum('bngrqk,bnkgd->bnqgrd', p, vb.astype(jnp.float32))
    o = o / jnp.moveaxis(denom, -1, 2)[..., None]
    lse = jnp.moveaxis(m + jnp.log(denom), -1, 2)
    return o.reshape(b, L, H, Dh), lse.reshape(b, L, H)


def _strided(t, dil, Lp):
    b, S, H, Dh = t.shape
    L = S // dil
    t = t.reshape(b, L, dil, H, Dh).transpose(0, 2, 1, 3, 4).reshape(b * dil, L, H, Dh)
    return jnp.pad(t, ((0, 0), (0, Lp - L), (0, 0), (0, 0)))


def dilated_mixture(q, k, v, slopes):
    b, S, H, Dh = q.shape
    outs, lses = [], []
    for window, dil in DILATED_BRANCHES:
        L = S // dil
        Lp = -(-L // BLOCK) * BLOCK
        o, lse = banded_attention(_strided(q, dil, Lp), _strided(k, dil, Lp),
                                  _strided(v, dil, Lp), window // dil, dil, slopes, None)
        outs.append(o[:, :L].reshape(b, dil, L, H, Dh).transpose(0, 2, 1, 3, 4).reshape(b, S, H, Dh))
        lses.append(lse[:, :L].reshape(b, dil, L, H).transpose(0, 2, 1, 3).reshape(b, S, H))
    w = jax.nn.softmax(jnp.stack(lses), axis=0)
    return jnp.einsum('nbsh,nbshd->bshd', w, jnp.stack(outs)).astype(q.dtype)


def setup_inputs(seed: int = 0) -> dict:
    key = jax.random.key(seed)
    ks = jax.random.split(key, 12)
    f32 = jnp.float32
    x = jax.random.normal(ks[0], (BATCH, SEQ, D_MODEL), f32)
    g_attn = 1.0 + 0.02 * jax.random.normal(ks[1], (DEPTH, D_MODEL), f32)
    w_in = jax.random.normal(ks[2], (DEPTH, D_MODEL, D_IN), f32) * D_MODEL ** -0.5
    b_in = 0.02 * jax.random.normal(ks[3], (DEPTH, D_IN), f32)
    sinks_a = jax.random.normal(ks[4], (DEPTH, N_HEADS_A), f32)
    g_out_a = 1.0 + 0.02 * jax.random.normal(ks[5], (DEPTH, Q_A), f32)
    g_out_b = 1.0 + 0.02 * jax.random.normal(ks[6], (DEPTH, Q_B), f32)
    w_out = jax.random.normal(ks[7], (DEPTH, D_MIX, D_MODEL), f32) * D_MIX ** -0.5
    g_mlp = 1.0 + 0.02 * jax.random.normal(ks[8], (DEPTH, D_MODEL), f32)
    w_1 = jax.random.normal(ks[9], (DEPTH, D_MODEL, D_FF), f32) * D_MODEL ** -0.5
    w_2 = jax.random.normal(ks[10], (DEPTH, D_FF, D_MODEL), f32) * D_FF ** -0.5
    g_final = 1.0 + 0.02 * jax.random.normal(ks[11], (D_MODEL,), f32)
    return {"x": x, "g_attn": g_attn, "w_in": w_in, "b_in": b_in, "sinks_a": sinks_a,
            "g_out_a": g_out_a, "g_out_b": g_out_b, "w_out": w_out, "g_mlp": g_mlp,
            "w_1": w_1, "w_2": w_2, "g_final": g_final}


def reference(x, g_attn, w_in, b_in, sinks_a, g_out_a, g_out_b, w_out, g_mlp, w_1, w_2, g_final):
    b, S, _ = x.shape
    slopes_a = alibi_slopes(N_HEADS_A)
    slopes_b = alibi_slopes(N_HEADS_B)
    for l in range(DEPTH):
        h = rmsnorm(x, g_attn[l])
        proj = jnp.einsum('bsd,de->bse', h, w_in[l]) + b_in[l]
        o1 = Q_A
        o2 = o1 + KV_A
        o3 = o2 + KV_A
        o4 = o3 + Q_B
        o5 = o4 + Q_B
        qa = proj[..., :o1].reshape(b, S, N_HEADS_A, HEAD_DIM)
        ka = proj[..., o1:o2].reshape(b, S, N_KV_A, HEAD_DIM)
        va = proj[..., o2:o3].reshape(b, S, N_KV_A, HEAD_DIM)
        qb = proj[..., o3:o4].reshape(b, S, N_HEADS_B, HEAD_DIM)
        kb = proj[..., o4:o5].reshape(b, S, N_HEADS_B, HEAD_DIM)
        vb = proj[..., o5:].reshape(b, S, N_HEADS_B, HEAD_DIM)
        oa, _ = banded_attention(qa, ka, va, WINDOW_A - 1, 1, slopes_a, sinks_a[l])
        ya = rmsnorm(oa.astype(x.dtype).reshape(b, S, Q_A), g_out_a[l])
        ob = dilated_mixture(qb, kb, vb, slopes_b)
        yb = rmsnorm(ob.reshape(b, S, Q_B), g_out_b[l])
        mix = jnp.concatenate([ya, yb], axis=-1)
        x = x + jnp.einsum('bse,ed->bsd', mix, w_out[l])
        h = rmsnorm(x, g_mlp[l])
        u = jax.nn.relu(jnp.einsum('bsd,df->bsf', h, w_1[l]))
        x = x + jnp.einsum('bsf,fd->bsd', u * u, w_2[l])
    return rmsnorm(x, g_final)
```

```python
import functools

import numpy as np
import jax
import jax.numpy as jnp
from jax import lax
from jax.experimental import pallas as pl
from jax.experimental.pallas import tpu as pltpu

HEAD_DIM = 64
N_HEADS = 16
N_KV_A = 2
BLOCK = 128
WINDOW_A = 128
DILATED_BRANCHES = ((128, 1), (512, 4), (2048, 16))
EPS = 1e-5
NEG_INF = -1e30
Q_W = N_HEADS * HEAD_DIM
KV_A_W = N_KV_A * HEAD_DIM
LANES = 128
VMEM_LIMIT = 56 * 1024 * 1024

bf16 = jnp.bfloat16
f32 = jnp.float32


def _alibi_slopes(n):
    return 2.0 ** (-8.0 * (np.arange(n) + 1) / n)


def _band_bias(max_steps, step_dist):
    qi = np.arange(BLOCK)[:, None]
    kj = np.arange(2 * BLOCK)[None, :]
    steps = qi + BLOCK - kj
    valid = (steps >= 0) & (steps <= max_steps)
    slopes = _alibi_slopes(N_HEADS).astype(np.float32)
    pen = slopes[:, None, None] * (step_dist * steps).astype(np.float32)[None]
    gen = np.where(valid[None], -pen, np.float32(NEG_INF)).astype(np.float32)
    first = np.where((valid & (kj >= BLOCK))[None], -pen, np.float32(NEG_INF)).astype(np.float32)
    return np.stack([gen, first])


def _inproj_kernel(x_ref, g_ref, w_ref, b_ref, qa_ref, ka_ref, va_ref, qb_ref, kb_ref, vb_ref,
                   *, chunk):
    x = x_ref[...]
    ms = jnp.mean(x * x, axis=-1, keepdims=True)
    h = ((x * lax.rsqrt(ms + EPS)) * g_ref[...]).astype(bf16)
    qscale = HEAD_DIM ** -0.5
    c0 = 0
    for ref, scale in ((qa_ref, qscale), (ka_ref, None), (va_ref, None),
                       (qb_ref, qscale), (kb_ref, None), (vb_ref, None)):
        width = ref.shape[-1]
        ch = min(chunk, width)
        for cc in range(0, width, ch):
            y = jnp.dot(h, w_ref[:, c0 + cc:c0 + cc + ch], preferred_element_type=f32)
            y = y + b_ref[:, c0 + cc:c0 + cc + ch]
            if scale is not None:
                y = y * scale
            ref[:, cc:cc + ch] = y.astype(bf16)
        c0 += width


def _inproj(x2d, g, w_bf, b, *, tm=512, chunk=512):
    T, D = x2d.shape
    d_in = w_bf.shape[1]
    widths = (Q_W, KV_A_W, KV_A_W, Q_W, Q_W, Q_W)
    assert sum(widths) == d_in and T % tm == 0
    const = lambda i: (0, 0)
    return pl.pallas_call(
        functools.partial(_inproj_kernel, chunk=chunk),
        grid=(T // tm,),
        in_specs=[
            pl.BlockSpec((tm, D), lambda i: (i, 0)),
            pl.BlockSpec((1, D), const),
            pl.BlockSpec((D, d_in), const, pipeline_mode=pl.Buffered(1)),
            pl.BlockSpec((1, d_in), const),
        ],
        out_specs=[pl.BlockSpec((tm, w), lambda i: (i, 0)) for w in widths],
        out_shape=[jax.ShapeDtypeStruct((T, w), bf16) for w in widths],
        compiler_params=pltpu.CompilerParams(
            dimension_semantics=("parallel",), vmem_limit_bytes=VMEM_LIMIT),
        name="inproj",
    )(x2d, g, w_bf, b)


def _attn_kernel(*refs, mode, tq):
    has_prev = mode in ("mid", "last")
    normed = mode in ("A", "last")
    it = iter(refs)
    q_ref, k_ref, kp_ref, v_ref, vp_ref, bias_ref = (next(it) for _ in range(6))
    sink_ref = next(it) if mode == "A" else None
    g_ref = next(it) if normed else None
    op_ref, lp_ref = (next(it), next(it)) if has_prev else (None, None)
    out_ref = next(it)
    lse_ref = None if normed else next(it)
    k_scr, v_scr = next(it), next(it)
    o_scr = next(it) if normed else out_ref.at[0]

    li = pl.program_id(2)
    nsub = tq // BLOCK
    lane = lax.broadcasted_iota(jnp.int32, (1, LANES), 1)
    lo = lane < HEAD_DIM

    if mode == "A":
        def dup(t):
            r = pltpu.roll(t.astype(f32), HEAD_DIM, 1).astype(bf16)
            return jnp.concatenate([jnp.where(lo, t, r), jnp.where(lo, r, t)], axis=1)
        k_scr[0:BLOCK, :] = dup(kp_ref[0])
        k_scr[BLOCK:, :] = dup(k_ref[0])
        v_scr[0:BLOCK, :] = dup(vp_ref[0])
        v_scr[BLOCK:, :] = dup(v_ref[0])
    else:
        k_scr[0:BLOCK, :] = kp_ref[0]
        k_scr[BLOCK:, :] = k_ref[0]
        v_scr[0:BLOCK, :] = vp_ref[0]
        v_scr[BLOCK:, :] = v_ref[0]
    if lse_ref is not None:
        lse_ref[0] = jnp.zeros(lse_ref.shape[1:], f32)

    def sub_block(j, carry):
        r0 = pl.multiple_of(j * BLOCK, BLOCK)
        rows = pl.ds(r0, BLOCK)
        krows = pl.ds(r0, 2 * BLOCK)
        sel = jnp.where(jnp.logical_and(li == 0, j == 0), 1, 0)
        for hp in range(N_HEADS // 2):
            cols = slice(hp * LANES, (hp + 1) * LANES)
            if mode == "A":
                g = (2 * hp) // (N_HEADS // N_KV_A)
                kcols = slice(g * LANES, (g + 1) * LANES)
            else:
                kcols = cols
            q2 = q_ref[0, rows, cols]
            k2 = k_scr[krows, kcols]
            v2 = v_scr[krows, kcols]
            zero = jnp.zeros_like(q2)
            pv, co, cp, ls = [], [], [], []
            for e in range(2):
                h = 2 * hp + e
                qm = jnp.where(lo, q2, zero) if e == 0 else jnp.where(lo, zero, q2)
                s = lax.dot_general(qm, k2, (((1,), (1,)), ((), ())),
                                    preferred_element_type=f32)
                s = s + bias_ref[sel, h]
                m = jnp.max(s, axis=-1, keepdims=True)
                if mode == "A":
                    sink = sink_ref[h]
                    m = jnp.maximum(m, sink)
                p = jnp.exp(s - m)
                l = jnp.sum(p, axis=-1, keepdims=True)
                if mode == "A":
                    l = l + jnp.exp(sink - m)
                pv.append(jnp.dot(p.astype(bf16), v2, preferred_element_type=f32))
                if has_prev:
                    lp = lp_ref[0, rows, h:h + 1]
                    mm = jnp.maximum(lp, m)
                    a = jnp.exp(lp - mm)
                    c = jnp.exp(m - mm)
                    den = a + c * l
                    inv = 1.0 / den
                    co.append(c * inv)
                    cp.append(a * inv)
                    ls.append(mm + jnp.log(den))
                else:
                    co.append(1.0 / l)
                    if lse_ref is not None:
                        ls.append(m + jnp.log(l))
            o2 = jnp.where(lo, pv[0] * co[0], pv[1] * co[1])
            if has_prev:
                o2 = o2 + jnp.where(lo, cp[0], cp[1]) * op_ref[0, rows, cols]
            o_scr[rows, cols] = o2
            if lse_ref is not None:
                for e in range(2):
                    h = 2 * hp + e
                    lse_ref[0, rows, h:h + 1] = ls[e]
        return carry

    lax.fori_loop(0, nsub, sub_block, 0)

    if normed:
        o = o_scr[...]
        ms = jnp.mean(o * o, axis=-1, keepdims=True)
        out_ref[0] = ((o * lax.rsqrt(ms + EPS)) * g_ref[...]).astype(bf16)


def _attention(mode, q, k, v, bias, *, dil, tq, sinks=None, gain=None, prev=None):
    b, S, _ = q.shape
    L = S // dil
    tq = min(tq, L)
    assert L % tq == 0 and tq % BLOCK == 0
    kw = k.shape[-1]
    nsub = tq // BLOCK
    has_prev = mode in ("mid", "last")
    normed = mode in ("A", "last")

    view = lambda t: t.reshape(b, L, dil * t.shape[-1])
    cur = lambda bi, r, li: (bi, li, r)
    prv = lambda bi, r, li: (bi, jnp.maximum(li * nsub - 1, 0), r)

    args = [view(q), view(k), view(k), view(v), view(v), bias]
    in_specs = [
        pl.BlockSpec((1, tq, Q_W), cur),
        pl.BlockSpec((1, tq, kw), cur),
        pl.BlockSpec((1, BLOCK, kw), prv),
        pl.BlockSpec((1, tq, kw), cur),
        pl.BlockSpec((1, BLOCK, kw), prv),
        pl.BlockSpec(bias.shape, lambda bi, r, li: (0, 0, 0, 0)),
    ]
    if mode == "A":
        args.append(sinks)
        in_specs.append(pl.BlockSpec(memory_space=pltpu.SMEM))
    if normed:
        args.append(gain)
        in_specs.append(pl.BlockSpec((1, Q_W), lambda bi, r, li: (0, 0)))
    if has_prev:
        o_prev, lse_prev = prev
        args += [view(o_prev), view(lse_prev)]
        in_specs += [pl.BlockSpec((1, tq, Q_W), cur), pl.BlockSpec((1, tq, LANES), cur)]

    scr_w = 2 * kw if mode == "A" else kw
    scratch = [pltpu.VMEM((tq + BLOCK, scr_w), bf16), pltpu.VMEM((tq + BLOCK, scr_w), bf16)]
    if normed:
        out_shape = jax.ShapeDtypeStruct((b, L, dil * Q_W), bf16)
        out_specs = pl.BlockSpec((1, tq, Q_W), cur)
        scratch.append(pltpu.VMEM((tq, Q_W), f32))
    else:
        out_shape = [jax.ShapeDtypeStruct((b, L, dil * Q_W), f32),
                     jax.ShapeDtypeStruct((b, L, dil * LANES), f32)]
        out_specs = [pl.BlockSpec((1, tq, Q_W), cur), pl.BlockSpec((1, tq, LANES), cur)]

    res = pl.pallas_call(
        functools.partial(_attn_kernel, mode=mode, tq=tq),
        grid=(b, dil, L // tq),
        in_specs=in_specs,
        out_specs=out_specs,
        out_shape=out_shape,
        scratch_shapes=scratch,
        compiler_params=pltpu.CompilerParams(
            dimension_semantics=("parallel", "parallel", "arbitrary"),
            vmem_limit_bytes=VMEM_LIMIT),
        name=f"attn_{mode}_d{dil}",
    )(*args)
    if normed:
        return res.reshape(b, S, Q_W)
    o, lse = res
    return o.reshape(b, S, Q_W), lse.reshape(b, S, LANES)


def _outproj_kernel(x_ref, ya_ref, yb_ref, w_ref, o_ref):
    acc = jnp.dot(ya_ref[...], w_ref[0:Q_W, :], preferred_element_type=f32)
    acc = acc + jnp.dot(yb_ref[...], w_ref[Q_W:, :], preferred_element_type=f32)
    o_ref[...] = x_ref[...] + acc


def _outproj(x2d, ya, yb, w_bf, *, tm=512):
    T, D = x2d.shape
    return pl.pallas_call(
        _outproj_kernel,
        grid=(T // tm,),
        in_specs=[
            pl.BlockSpec((tm, D), lambda i: (i, 0)),
            pl.BlockSpec((tm, Q_W), lambda i: (i, 0)),
            pl.BlockSpec((tm, Q_W), lambda i: (i, 0)),
            pl.BlockSpec(w_bf.shape, lambda i: (0, 0), pipeline_mode=pl.Buffered(1)),
        ],
        out_specs=pl.BlockSpec((tm, D), lambda i: (i, 0)),
        out_shape=jax.ShapeDtypeStruct((T, D), f32),
        compiler_params=pltpu.CompilerParams(
            dimension_semantics=("parallel",), vmem_limit_bytes=VMEM_LIMIT),
        name="outproj",
    )(x2d, ya, yb, w_bf)


def _mlp_kernel(x_ref, g_ref, w1_ref, w2_ref, gf_ref, o_ref, h_scr):
    f = pl.program_id(1)

    @pl.when(f == 0)
    def _():
        x = x_ref[...]
        ms = jnp.mean(x * x, axis=-1, keepdims=True)
        h_scr[...] = ((x * lax.rsqrt(ms + EPS)) * g_ref[...]).astype(bf16)
        o_ref[...] = x

    u = jnp.dot(h_scr[...], w1_ref[...], preferred_element_type=f32)
    u = jnp.maximum(u, 0.0)
    o_ref[...] += jnp.dot((u * u).astype(bf16), w2_ref[...], preferred_element_type=f32)

    @pl.when(f == pl.num_programs(1) - 1)
    def _():
        y = o_ref[...]
        ms = jnp.mean(y * y, axis=-1, keepdims=True)
        o_ref[...] = (y * lax.rsqrt(ms + EPS)) * gf_ref[...]


def _mlp(x2d, g, w1_bf, w2_bf, gf, *, tm=1024, tf=512):
    T, D = x2d.shape
    d_ff = w1_bf.shape[1]
    tm = min(tm, T)
    assert T % tm == 0 and d_ff % tf == 0
    return pl.pallas_call(
        _mlp_kernel,
        grid=(T // tm, d_ff // tf),
        in_specs=[
            pl.BlockSpec((tm, D), lambda i, f: (i, 0)),
            pl.BlockSpec((1, D), lambda i, f: (0, 0)),
            pl.BlockSpec((D, tf), lambda i, f: (0, f)),
            pl.BlockSpec((tf, D), lambda i, f: (f, 0)),
            pl.BlockSpec((1, D), lambda i, f: (0, 0)),
        ],
        out_specs=pl.BlockSpec((tm, D), lambda i, f: (i, 0)),
        out_shape=jax.ShapeDtypeStruct((T, D), f32),
        scratch_shapes=[pltpu.VMEM((tm, D), bf16)],
        compiler_params=pltpu.CompilerParams(
            dimension_semantics=("parallel", "arbitrary"), vmem_limit_bytes=VMEM_LIMIT),
        name="mlp",
    )(x2d, g, w1_bf, w2_bf, gf)


def kernel(x, g_attn, w_in, b_in, sinks_a, g_out_a, g_out_b, w_out, g_mlp, w_1, w_2, g_final):
    b, S, D = x.shape
    depth = w_in.shape[0]
    T = b * S
    bias_a = jnp.asarray(_band_bias(WINDOW_A - 1, 1))
    bias_b = [jnp.asarray(_band_bias(w // d, d)) for w, d in DILATED_BRANCHES]
    modes = ("first", "mid", "last")

    x2d = x.reshape(T, D)
    for l in range(depth):
        last_layer = l == depth - 1
        qa, ka, va, qb, kb, vb = _inproj(x2d, g_attn[l][None], w_in[l].astype(bf16), b_in[l][None])
        r3 = lambda t: t.reshape(b, S, t.shape[-1])
        ya = _attention("A", r3(qa), r3(ka), r3(va), bias_a, dil=1, tq=512,
                        sinks=sinks_a[l], gain=g_out_a[l][None])
        state = None
        for (_, dil), mode, bias in zip(DILATED_BRANCHES, modes, bias_b):
            state = _attention(mode, r3(qb), r3(kb), r3(vb), bias, dil=dil, tq=512,
                               gain=g_out_b[l][None] if mode == "last" else None, prev=state)
        yb = state
        x1 = _outproj(x2d, ya.reshape(T, Q_W), yb.reshape(T, Q_W), w_out[l].astype(bf16))
        gf = g_final[None] if last_layer else None
        assert last_layer, "final norm is fused into the last layer's MLP kernel"
        x2d = _mlp(x1, g_mlp[l][None], w_1[l].astype(bf16), w_2[l].astype(bf16), gf)
    return x2d.reshape(b, S, D)
```

```python
import functools

import numpy as np
import jax
import jax.numpy as jnp
from jax import lax
from jax.experimental import pallas as pl
from jax.experimental.pallas import tpu as pltpu

HEAD_DIM = 64
N_HEADS = 16
N_KV_A = 2
BLOCK = 128
WINDOW_A = 128
DILATED_BRANCHES = ((128, 1), (512, 4), (2048, 16))
EPS = 1e-5
NEG_INF = -1e30
Q_W = N_HEADS * HEAD_DIM
KV_A_W = N_KV_A * HEAD_DIM
LANES = 128
VMEM_LIMIT = 56 * 1024 * 1024

bf16 = jnp.bfloat16
f32 = jnp.float32
_NT = (((1,), (1,)), ((), ()))


def _alibi_slopes(n):
    return 2.0 ** (-8.0 * (np.arange(n) + 1) / n)


def _band_bias(max_steps, step_dist):
    qi = np.arange(BLOCK)[:, None]
    kj = np.arange(2 * BLOCK)[None, :]
    steps = qi + BLOCK - kj
    valid = (steps >= 0) & (steps <= max_steps)
    slopes = _alibi_slopes(N_HEADS).astype(np.float32)
    pen = slopes[:, None, None] * (step_dist * steps).astype(np.float32)[None]
    gen = np.where(valid[None], -pen, np.float32(NEG_INF)).astype(np.float32)
    first = np.where((valid & (kj >= BLOCK))[None], -pen, np.float32(NEG_INF)).astype(np.float32)
    return np.stack([gen, first])


def _inproj_kernel(x_ref, g_ref, w_ref, b_ref, qa_ref, ka_ref, va_ref, qb_ref, kb_ref, vb_ref,
                   *, chunk):
    x = x_ref[...]
    ms = jnp.mean(x * x, axis=-1, keepdims=True)
    h = ((x * lax.rsqrt(ms + EPS)) * g_ref[...]).astype(bf16)
    qscale = HEAD_DIM ** -0.5
    c0 = 0
    for ref, scale in ((qa_ref, qscale), (ka_ref, None), (va_ref, None),
                       (qb_ref, qscale), (kb_ref, None), (vb_ref, None)):
        width = ref.shape[-1]
        ch = min(chunk, width)
        for cc in range(0, width, ch):
            y = jnp.dot(h, w_ref[:, c0 + cc:c0 + cc + ch], preferred_element_type=f32)
            y = y + b_ref[:, c0 + cc:c0 + cc + ch]
            if scale is not None:
                y = y * scale
            ref[:, cc:cc + ch] = y.astype(bf16)
        c0 += width


def _inproj(x2d, g, w_bf, b, *, tm=512, chunk=512):
    T, D = x2d.shape
    d_in = w_bf.shape[1]
    widths = (Q_W, KV_A_W, KV_A_W, Q_W, Q_W, Q_W)
    assert sum(widths) == d_in and T % tm == 0
    const = lambda i: (0, 0)
    return pl.pallas_call(
        functools.partial(_inproj_kernel, chunk=chunk),
        grid=(T // tm,),
        in_specs=[
            pl.BlockSpec((tm, D), lambda i: (i, 0)),
            pl.BlockSpec((1, D), const),
            pl.BlockSpec((D, d_in), const, pipeline_mode=pl.Buffered(1)),
            pl.BlockSpec((1, d_in), const),
        ],
        out_specs=[pl.BlockSpec((tm, w), lambda i: (i, 0)) for w in widths],
        out_shape=[jax.ShapeDtypeStruct((T, w), bf16) for w in widths],
        compiler_params=pltpu.CompilerParams(
            dimension_semantics=("parallel",), vmem_limit_bytes=VMEM_LIMIT),
        name="inproj",
    )(x2d, g, w_bf, b)


def _attn_a_kernel(q_ref, k_ref, kp_ref, v_ref, vp_ref, bias_ref, sink_ref, g_ref, out_ref,
                   k_scr, v_scr, o_scr, *, tq):
    li = pl.program_id(1)
    lane = lax.broadcasted_iota(jnp.int32, (1, LANES), 1)
    lo = lane < HEAD_DIM

    def dup(t):
        r = pltpu.roll(t.astype(f32), HEAD_DIM, 1).astype(bf16)
        return jnp.concatenate([jnp.where(lo, t, r), jnp.where(lo, r, t)], axis=1)
    k_scr[0:BLOCK, :] = dup(kp_ref[0])
    k_scr[BLOCK:, :] = dup(k_ref[0])
    v_scr[0:BLOCK, :] = dup(vp_ref[0])
    v_scr[BLOCK:, :] = dup(v_ref[0])

    def sub_block(j, carry):
        r0 = pl.multiple_of(j * BLOCK, BLOCK)
        rows = pl.ds(r0, BLOCK)
        krows = pl.ds(r0, 2 * BLOCK)
        sel = jnp.where(jnp.logical_and(li == 0, j == 0), 1, 0)
        for hp in range(N_HEADS // 2):
            cols = slice(hp * LANES, (hp + 1) * LANES)
            g = (2 * hp) // (N_HEADS // N_KV_A)
            kcols = slice(g * LANES, (g + 1) * LANES)
            q2 = q_ref[0, rows, cols]
            k2 = k_scr[krows, kcols]
            v2 = v_scr[krows, kcols]
            zero = jnp.zeros_like(q2)
            o = []
            for e in range(2):
                h = 2 * hp + e
                qm = jnp.where(lo, q2, zero) if e == 0 else jnp.where(lo, zero, q2)
                s = lax.dot_general(qm, k2, _NT, preferred_element_type=f32)
                s = s + bias_ref[sel, h]
                sink = sink_ref[h]
                m = jnp.maximum(jnp.max(s, axis=-1, keepdims=True), sink)
                p = jnp.exp(s - m)
                l = jnp.sum(p, axis=-1, keepdims=True) + jnp.exp(sink - m)
                pv = jnp.dot(p.astype(bf16), v2, preferred_element_type=f32)
                o.append(pv * (1.0 / l))
            o_scr[rows, cols] = jnp.where(lo, o[0], o[1])
        return carry

    lax.fori_loop(0, tq // BLOCK, sub_block, 0)

    o = o_scr[...]
    ms = jnp.mean(o * o, axis=-1, keepdims=True)
    out_ref[0] = ((o * lax.rsqrt(ms + EPS)) * g_ref[...]).astype(bf16)


def _attention_a(q, k, v, bias, sinks, gain, *, tq=512):
    b, S, _ = q.shape
    tq = min(tq, S)
    assert S % tq == 0 and tq % BLOCK == 0
    nsub = tq // BLOCK
    cur = lambda bi, li: (bi, li, 0)
    prv = lambda bi, li: (bi, jnp.maximum(li * nsub - 1, 0), 0)
    return pl.pallas_call(
        functools.partial(_attn_a_kernel, tq=tq),
        grid=(b, S // tq),
        in_specs=[
            pl.BlockSpec((1, tq, Q_W), cur),
            pl.BlockSpec((1, tq, KV_A_W), cur),
            pl.BlockSpec((1, BLOCK, KV_A_W), prv),
            pl.BlockSpec((1, tq, KV_A_W), cur),
            pl.BlockSpec((1, BLOCK, KV_A_W), prv),
            pl.BlockSpec(bias.shape, lambda bi, li: (0, 0, 0, 0)),
            pl.BlockSpec(memory_space=pltpu.SMEM),
            pl.BlockSpec((1, Q_W), lambda bi, li: (0, 0)),
        ],
        out_specs=pl.BlockSpec((1, tq, Q_W), cur),
        out_shape=jax.ShapeDtypeStruct((b, S, Q_W), bf16),
        scratch_shapes=[pltpu.VMEM((tq + BLOCK, 2 * KV_A_W), bf16),
                        pltpu.VMEM((tq + BLOCK, 2 * KV_A_W), bf16),
                        pltpu.VMEM((tq, Q_W), f32)],
        compiler_params=pltpu.CompilerParams(
            dimension_semantics=("parallel", "arbitrary"), vmem_limit_bytes=VMEM_LIMIT),
        name="attn_a",
    )(q, k, k, v, v, bias, sinks, gain)


B_TILE = 2048
B_DILS = tuple(d for _, d in DILATED_BRANCHES)
assert B_DILS == (1, 4, 16) and all(w // d == BLOCK for w, d in DILATED_BRANCHES)
B_NSUB = B_TILE // BLOCK
B_HALO4 = BLOCK * 4
B_ROWS4 = (B_HALO4 + B_TILE) // 4
B_ROWS16 = 2 * B_TILE // 16


_CHUNK = 256


def _deinterleave(stage, stage4, n_tok, off4, d4, d16):
    s4 = stage4.shape[0] // 4
    n4 = n_tok // 4
    for r in range(4):
        for c in range(0, n4, _CHUNK):
            stage4[r * s4 + c:r * s4 + c + _CHUNK, :] = stage[pl.ds(4 * c + r, _CHUNK, stride=4), :]
    rows4 = n4 - off4
    for r in range(4):
        for c in range(0, rows4, BLOCK):
            d4[r * rows4 + c:r * rows4 + c + BLOCK, :] = stage4[
                r * s4 + off4 + c:r * s4 + off4 + c + BLOCK, :].astype(bf16)
    rows16 = n4 // 4
    for r in range(4):
        for c in range(4):
            for cc in range(0, rows16, BLOCK):
                d16[(r + 4 * c) * rows16 + cc:(r + 4 * c) * rows16 + cc + BLOCK, :] = stage4[
                    pl.ds(r * s4 + c + 4 * cc, BLOCK, stride=4), :].astype(bf16)


def _attn_b_kernel(q_ref, k_ref, kh_ref, v_ref, vh_ref, bias_ref, o_ref,
                   stage, stage4, q4, q16, k1, v1, k4, v4, k16, v16, pv_st, l_st, m_st):
    ti = pl.program_id(2)
    lane = lax.broadcasted_iota(jnp.int32, (1, LANES), 1)
    lo = lane < HEAD_DIM
    ones = jnp.ones((2 * BLOCK, LANES), bf16)

    for cur, halo, d1, d4, d16 in ((k_ref, kh_ref, k1, k4, k16), (v_ref, vh_ref, v1, v4, v16)):
        d1[0:BLOCK, :] = halo[0, B_TILE - BLOCK:B_TILE, :]
        d1[BLOCK:, :] = cur[0]
        for c in range(0, B_TILE, _CHUNK):
            stage[c:c + _CHUNK, :] = halo[0, c:c + _CHUNK, :].astype(f32)
            stage[B_TILE + c:B_TILE + c + _CHUNK, :] = cur[0, c:c + _CHUNK, :].astype(f32)
        _deinterleave(stage, stage4, 2 * B_TILE, (B_TILE - B_HALO4) // 4, d4, d16)
    for c in range(0, B_TILE, _CHUNK):
        stage[c:c + _CHUNK, :] = q_ref[0, c:c + _CHUNK, :].astype(f32)
    _deinterleave(stage, stage4, B_TILE, 0, q4, q16)

    def pair_block(br, q2, k2, v2, sel, rows):
        v2a = jnp.concatenate([v2, ones], axis=1)
        zero = jnp.zeros_like(q2)
        pv, l, m = [], [], []
        for e in range(2):
            qm = jnp.where(lo, q2, zero) if e == 0 else jnp.where(lo, zero, q2)
            s = lax.dot_general(qm, k2, _NT, preferred_element_type=f32)
            s = s + bias_ref[sel, br, e]
            me = jnp.max(s, axis=-1, keepdims=True)
            p = jnp.exp(s - me)
            pva = jnp.dot(p.astype(bf16), v2a, preferred_element_type=f32)
            pv.append(pva[:, :LANES])
            l.append(pva[:, LANES:])
            m.append(jnp.broadcast_to(me, (BLOCK, LANES)))
        pv_st[br, rows, :] = jnp.where(lo, pv[0], pv[1])
        l_st[br, rows, :] = jnp.where(lo, l[0], l[1])
        m_st[br, rows, :] = jnp.where(lo, m[0], m[1])

    def first_sel(cond):
        return jnp.where(cond, 1, 0)

    def branch1(j, c):
        r0 = pl.multiple_of(j * BLOCK, BLOCK)
        pair_block(0, q_ref[0, pl.ds(r0, BLOCK), :], k1[pl.ds(r0, 2 * BLOCK), :],
                   v1[pl.ds(r0, 2 * BLOCK), :],
                   first_sel(jnp.logical_and(ti == 0, j == 0)), pl.ds(r0, BLOCK))
        return c

    def branch4(i, c):
        r, j = i // 4, i % 4
        q0 = pl.multiple_of(r * (B_TILE // 4) + j * BLOCK, BLOCK)
        k0 = pl.multiple_of(r * B_ROWS4 + j * BLOCK, BLOCK)
        pair_block(1, q4[pl.ds(q0, BLOCK), :], k4[pl.ds(k0, 2 * BLOCK), :],
                   v4[pl.ds(k0, 2 * BLOCK), :],
                   first_sel(jnp.logical_and(ti == 0, j == 0)),
                   pl.ds(j * (4 * BLOCK) + r, BLOCK, stride=4))
        return c

    def branch16(r, c):
        q0 = pl.multiple_of(r * BLOCK, BLOCK)
        k0 = pl.multiple_of(r * B_ROWS16, BLOCK)
        pair_block(2, q16[pl.ds(q0, BLOCK), :], k16[pl.ds(k0, 2 * BLOCK), :],
                   v16[pl.ds(k0, 2 * BLOCK), :], first_sel(ti == 0),
                   pl.ds(r, BLOCK, stride=16))
        return c

    lax.fori_loop(0, B_NSUB, branch1, 0, unroll=4)
    lax.fori_loop(0, B_NSUB, branch4, 0, unroll=4)
    lax.fori_loop(0, B_NSUB, branch16, 0, unroll=4)

    def combine(c, carry):
        rows = pl.ds(pl.multiple_of(c * BLOCK, BLOCK), BLOCK)
        ms = [m_st[n, rows, :] for n in range(3)]
        mx = jnp.maximum(jnp.maximum(ms[0], ms[1]), ms[2])
        num = den = None
        for n in range(3):
            w = jnp.exp(ms[n] - mx)
            tn, td = w * pv_st[n, rows, :], w * l_st[n, rows, :]
            num = tn if num is None else num + tn
            den = td if den is None else den + td
        o_ref[0, rows, :] = num / den
        return carry

    lax.fori_loop(0, B_NSUB, combine, 0)


def _attention_b(q, k, v, bias):
    b, S, _ = q.shape
    assert S % B_TILE == 0
    npair = N_HEADS // 2
    cur = lambda bi, hp, ti: (bi, ti, hp)
    halo = lambda bi, hp, ti: (bi, jnp.maximum(ti - 1, 0), hp)
    blk = pl.BlockSpec((1, B_TILE, LANES), cur)
    hblk = pl.BlockSpec((1, B_TILE, LANES), halo)
    vm = lambda rows, dt: pltpu.VMEM((rows, LANES), dt)
    return pl.pallas_call(
        _attn_b_kernel,
        grid=(b, npair, S // B_TILE),
        in_specs=[blk, blk, hblk, blk, hblk,
                  pl.BlockSpec((2, 3, 2, BLOCK, 2 * BLOCK), lambda bi, hp, ti: (0, 0, hp, 0, 0))],
        out_specs=blk,
        out_shape=jax.ShapeDtypeStruct((b, S, Q_W), f32),
        scratch_shapes=[
            vm(2 * B_TILE, f32), vm(2 * B_TILE, f32),
            vm(B_TILE, bf16), vm(B_TILE, bf16),
            vm(B_TILE + BLOCK, bf16), vm(B_TILE + BLOCK, bf16),
            vm(4 * B_ROWS4, bf16), vm(4 * B_ROWS4, bf16),
            vm(16 * B_ROWS16, bf16), vm(16 * B_ROWS16, bf16),
            pltpu.VMEM((3, B_TILE, LANES), f32),
            pltpu.VMEM((3, B_TILE, LANES), f32),
            pltpu.VMEM((3, B_TILE, LANES), f32),
        ],
        compiler_params=pltpu.CompilerParams(
            dimension_semantics=("parallel", "parallel", "arbitrary"),
            vmem_limit_bytes=VMEM_LIMIT),
        name="attn_b",
    )(q, k, k, v, v, bias)


def _outproj_kernel(x_ref, ya_ref, ob_ref, g_ref, w_ref, o_ref):
    ob = ob_ref[...]
    ms = jnp.mean(ob * ob, axis=-1, keepdims=True)
    yb = ((ob * lax.rsqrt(ms + EPS)) * g_ref[...]).astype(bf16)
    acc = jnp.dot(ya_ref[...], w_ref[0:Q_W, :], preferred_element_type=f32)
    acc = acc + jnp.dot(yb, w_ref[Q_W:, :], preferred_element_type=f32)
    o_ref[...] = x_ref[...] + acc


def _outproj(x2d, ya, ob, g_b, w_bf, *, tm=512):
    T, D = x2d.shape
    return pl.pallas_call(
        _outproj_kernel,
        grid=(T // tm,),
        in_specs=[
            pl.BlockSpec((tm, D), lambda i: (i, 0)),
            pl.BlockSpec((tm, Q_W), lambda i: (i, 0)),
            pl.BlockSpec((tm, Q_W), lambda i: (i, 0)),
            pl.BlockSpec((1, Q_W), lambda i: (0, 0)),
            pl.BlockSpec(w_bf.shape, lambda i: (0, 0), pipeline_mode=pl.Buffered(1)),
        ],
        out_specs=pl.BlockSpec((tm, D), lambda i: (i, 0)),
        out_shape=jax.ShapeDtypeStruct((T, D), f32),
        compiler_params=pltpu.CompilerParams(
            dimension_semantics=("parallel",), vmem_limit_bytes=VMEM_LIMIT),
        name="outproj",
    )(x2d, ya, ob, g_b, w_bf)


def _mlp_kernel(x_ref, g_ref, w1_ref, w2_ref, gf_ref, o_ref, h_scr, *, final_norm):
    f = pl.program_id(1)

    @pl.when(f == 0)
    def _():
        x = x_ref[...]
        ms = jnp.mean(x * x, axis=-1, keepdims=True)
        h_scr[...] = ((x * lax.rsqrt(ms + EPS)) * g_ref[...]).astype(bf16)
        o_ref[...] = x

    u = jnp.dot(h_scr[...], w1_ref[...], preferred_element_type=f32)
    u = jnp.maximum(u, 0.0)
    o_ref[...] += jnp.dot((u * u).astype(bf16), w2_ref[...], preferred_element_type=f32)

    if final_norm:
        @pl.when(f == pl.num_programs(1) - 1)
        def _():
            y = o_ref[...]
            ms = jnp.mean(y * y, axis=-1, keepdims=True)
            o_ref[...] = (y * lax.rsqrt(ms + EPS)) * gf_ref[...]


def _mlp(x2d, g, w1_bf, w2_bf, gf, *, final_norm, tm=1024, tf=512):
    T, D = x2d.shape
    d_ff = w1_bf.shape[1]
    tm = min(tm, T)
    assert T % tm == 0 and d_ff % tf == 0
    return pl.pallas_call(
        functools.partial(_mlp_kernel, final_norm=final_norm),
        grid=(T // tm, d_ff // tf),
        in_specs=[
            pl.BlockSpec((tm, D), lambda i, f: (i, 0)),
            pl.BlockSpec((1, D), lambda i, f: (0, 0)),
            pl.BlockSpec((D, tf), lambda i, f: (0, f)),
            pl.BlockSpec((tf, D), lambda i, f: (f, 0)),
            pl.BlockSpec((1, D), lambda i, f: (0, 0)),
        ],
        out_specs=pl.BlockSpec((tm, D), lambda i, f: (i, 0)),
        out_shape=jax.ShapeDtypeStruct((T, D), f32),
        scratch_shapes=[pltpu.VMEM((tm, D), bf16)],
        compiler_params=pltpu.CompilerParams(
            dimension_semantics=("parallel", "arbitrary"), vmem_limit_bytes=VMEM_LIMIT),
        name="mlp",
    )(x2d, g, w1_bf, w2_bf, gf)


def kernel(x, g_attn, w_in, b_in, sinks_a, g_out_a, g_out_b, w_out, g_mlp, w_1, w_2, g_final):
    b, S, D = x.shape
    depth = w_in.shape[0]
    T = b * S
    bias_a = jnp.asarray(_band_bias(WINDOW_A - 1, 1))
    bias_b = jnp.asarray(np.stack([_band_bias(w // d, d) for w, d in DILATED_BRANCHES], axis=1))

    x2d = x.reshape(T, D)
    for l in range(depth):
        qa, ka, va, qb, kb, vb = _inproj(x2d, g_attn[l][None], w_in[l].astype(bf16), b_in[l][None])
        r3 = lambda t: t.reshape(b, S, t.shape[-1])
        ya = _attention_a(r3(qa), r3(ka), r3(va), bias_a, sinks_a[l], g_out_a[l][None])
        ob = _attention_b(r3(qb), r3(kb), r3(vb), bias_b)
        x1 = _outproj(x2d, ya.reshape(T, Q_W), ob.reshape(T, Q_W), g_out_b[l][None],
                      w_out[l].astype(bf16))
        x2d = _mlp(x1, g_mlp[l][None], w_1[l].astype(bf16), w_2[l].astype(bf16), g_final[None],
                   final_norm=(l == depth - 1))
    return x2d.reshape(b, S, D)
```

```python
import functools

import numpy as np
import jax
import jax.numpy as jnp
from jax import lax
from jax.experimental import pallas as pl
from jax.experimental.pallas import tpu as pltpu

HEAD_DIM = 64
N_HEADS = 16
N_KV_A = 2
BLOCK = 128
WINDOW_A = 128
DILATED_BRANCHES = ((128, 1), (512, 4), (2048, 16))
EPS = 1e-5
NEG_INF = -1e30
Q_W = N_HEADS * HEAD_DIM
KV_A_W = N_KV_A * HEAD_DIM
LANES = 128
VMEM_LIMIT = 56 * 1024 * 1024

bf16 = jnp.bfloat16
f32 = jnp.float32
_NT = (((1,), (1,)), ((), ()))


def _alibi_slopes(n):
    return 2.0 ** (-8.0 * (np.arange(n) + 1) / n)


def _band_bias(max_steps, step_dist):
    qi = np.arange(BLOCK)[:, None]
    kj = np.arange(2 * BLOCK)[None, :]
    steps = qi + BLOCK - kj
    valid = (steps >= 0) & (steps <= max_steps)
    slopes = _alibi_slopes(N_HEADS).astype(np.float32)
    pen = slopes[:, None, None] * (step_dist * steps).astype(np.float32)[None]
    gen = np.where(valid[None], -pen, np.float32(NEG_INF)).astype(np.float32)
    first = np.where((valid & (kj >= BLOCK))[None], -pen, np.float32(NEG_INF)).astype(np.float32)
    return np.stack([gen, first])


def _inproj_kernel(x_ref, g_ref, w_ref, b_ref, qa_ref, ka_ref, va_ref, qb_ref, kb_ref, vb_ref,
                   *, chunk):
    x = x_ref[...]
    ms = jnp.mean(x * x, axis=-1, keepdims=True)
    h = ((x * lax.rsqrt(ms + EPS)) * g_ref[...]).astype(bf16)
    qscale = HEAD_DIM ** -0.5
    c0 = 0
    for ref, scale in ((qa_ref, qscale), (ka_ref, None), (va_ref, None),
                       (qb_ref, qscale), (kb_ref, None), (vb_ref, None)):
        width = ref.shape[-1]
        ch = min(chunk, width)
        for cc in range(0, width, ch):
            y = jnp.dot(h, w_ref[:, c0 + cc:c0 + cc + ch], preferred_element_type=f32)
            y = y + b_ref[:, c0 + cc:c0 + cc + ch]
            if scale is not None:
                y = y * scale
            ref[:, cc:cc + ch] = y.astype(bf16)
        c0 += width


def _inproj(x2d, g, w_bf, b, *, tm=512, chunk=512):
    T, D = x2d.shape
    d_in = w_bf.shape[1]
    widths = (Q_W, KV_A_W, KV_A_W, Q_W, Q_W, Q_W)
    assert sum(widths) == d_in and T % tm == 0
    const = lambda i: (0, 0)
    return pl.pallas_call(
        functools.partial(_inproj_kernel, chunk=chunk),
        grid=(T // tm,),
        in_specs=[
            pl.BlockSpec((tm, D), lambda i: (i, 0)),
            pl.BlockSpec((1, D), const),
            pl.BlockSpec((D, d_in), const, pipeline_mode=pl.Buffered(1)),
            pl.BlockSpec((1, d_in), const),
        ],
        out_specs=[pl.BlockSpec((tm, w), lambda i: (i, 0)) for w in widths],
        out_shape=[jax.ShapeDtypeStruct((T, w), bf16) for w in widths],
        compiler_params=pltpu.CompilerParams(
            dimension_semantics=("parallel",), vmem_limit_bytes=VMEM_LIMIT),
        name="inproj",
    )(x2d, g, w_bf, b)


def _attn_a_kernel(q_ref, k_ref, kp_ref, v_ref, vp_ref, bias_ref, sink_ref, g_ref, out_ref,
                   k_scr, v_scr, o_scr, *, tq):
    li = pl.program_id(1)
    lane = lax.broadcasted_iota(jnp.int32, (1, LANES), 1)
    lo = lane < HEAD_DIM

    def dup(t):
        r = pltpu.roll(t.astype(f32), HEAD_DIM, 1).astype(bf16)
        return jnp.concatenate([jnp.where(lo, t, r), jnp.where(lo, r, t)], axis=1)
    k_scr[0:BLOCK, :] = dup(kp_ref[0])
    k_scr[BLOCK:, :] = dup(k_ref[0])
    v_scr[0:BLOCK, :] = dup(vp_ref[0])
    v_scr[BLOCK:, :] = dup(v_ref[0])

    def sub_block(j, carry):
        r0 = pl.multiple_of(j * BLOCK, BLOCK)
        rows = pl.ds(r0, BLOCK)
        krows = pl.ds(r0, 2 * BLOCK)
        sel = jnp.where(jnp.logical_and(li == 0, j == 0), 1, 0)
        for hp in range(N_HEADS // 2):
            cols = slice(hp * LANES, (hp + 1) * LANES)
            g = (2 * hp) // (N_HEADS // N_KV_A)
            kcols = slice(g * LANES, (g + 1) * LANES)
            q2 = q_ref[0, rows, cols]
            k2 = k_scr[krows, kcols]
            v2 = v_scr[krows, kcols]
            zero = jnp.zeros_like(q2)
            o = []
            for e in range(2):
                h = 2 * hp + e
                qm = jnp.where(lo, q2, zero) if e == 0 else jnp.where(lo, zero, q2)
                s = lax.dot_general(qm, k2, _NT, preferred_element_type=f32)
                s = s + bias_ref[sel, h]
                sink = sink_ref[h]
                m = jnp.maximum(jnp.max(s, axis=-1, keepdims=True), sink)
                p = jnp.exp(s - m)
                l = jnp.sum(p, axis=-1, keepdims=True) + jnp.exp(sink - m)
                pv = jnp.dot(p.astype(bf16), v2, preferred_element_type=f32)
                o.append(pv * (1.0 / l))
            o_scr[rows, cols] = jnp.where(lo, o[0], o[1])
        return carry

    lax.fori_loop(0, tq // BLOCK, sub_block, 0, unroll=2)

    o = o_scr[...]
    ms = jnp.mean(o * o, axis=-1, keepdims=True)
    out_ref[0] = ((o * lax.rsqrt(ms + EPS)) * g_ref[...]).astype(bf16)


def _attention_a(q, k, v, bias, sinks, gain, *, tq=512):
    b, S, _ = q.shape
    tq = min(tq, S)
    assert S % tq == 0 and tq % BLOCK == 0
    nsub = tq // BLOCK
    cur = lambda bi, li: (bi, li, 0)
    prv = lambda bi, li: (bi, jnp.maximum(li * nsub - 1, 0), 0)
    return pl.pallas_call(
        functools.partial(_attn_a_kernel, tq=tq),
        grid=(b, S // tq),
        in_specs=[
            pl.BlockSpec((1, tq, Q_W), cur),
            pl.BlockSpec((1, tq, KV_A_W), cur),
            pl.BlockSpec((1, BLOCK, KV_A_W), prv),
            pl.BlockSpec((1, tq, KV_A_W), cur),
            pl.BlockSpec((1, BLOCK, KV_A_W), prv),
            pl.BlockSpec(bias.shape, lambda bi, li: (0, 0, 0, 0)),
            pl.BlockSpec(memory_space=pltpu.SMEM),
            pl.BlockSpec((1, Q_W), lambda bi, li: (0, 0)),
        ],
        out_specs=pl.BlockSpec((1, tq, Q_W), cur),
        out_shape=jax.ShapeDtypeStruct((b, S, Q_W), bf16),
        scratch_shapes=[pltpu.VMEM((tq + BLOCK, 2 * KV_A_W), bf16),
                        pltpu.VMEM((tq + BLOCK, 2 * KV_A_W), bf16),
                        pltpu.VMEM((tq, Q_W), f32)],
        compiler_params=pltpu.CompilerParams(
            dimension_semantics=("parallel", "arbitrary"), vmem_limit_bytes=VMEM_LIMIT),
        name="attn_a",
    )(q, k, k, v, v, bias, sinks, gain)


B_TILE = 2048
B_DILS = tuple(d for _, d in DILATED_BRANCHES)
assert B_DILS == (1, 4, 16) and all(w // d == BLOCK for w, d in DILATED_BRANCHES)
B_NSUB = B_TILE // BLOCK
B_HALO4 = BLOCK * 4
B_ROWS4 = (B_HALO4 + B_TILE) // 4
B_ROWS16 = 2 * B_TILE // 16


_CHUNK = 256


def _deinterleave(stage, stage4, n_tok, off4, d4, d16):
    s4 = stage4.shape[0] // 4
    n4 = n_tok // 4
    for r in range(4):
        for c in range(0, n4, _CHUNK):
            stage4[r * s4 + c:r * s4 + c + _CHUNK, :] = stage[pl.ds(4 * c + r, _CHUNK, stride=4), :]
    rows4 = n4 - off4
    for r in range(4):
        for c in range(0, rows4, BLOCK):
            d4[r * rows4 + c:r * rows4 + c + BLOCK, :] = stage4[
                r * s4 + off4 + c:r * s4 + off4 + c + BLOCK, :].astype(bf16)
    rows16 = n4 // 4
    for r in range(4):
        for c in range(4):
            for cc in range(0, rows16, BLOCK):
                d16[(r + 4 * c) * rows16 + cc:(r + 4 * c) * rows16 + cc + BLOCK, :] = stage4[
                    pl.ds(r * s4 + c + 4 * cc, BLOCK, stride=4), :].astype(bf16)


def _attn_b_kernel(q_ref, k_ref, kh_ref, v_ref, vh_ref, bias_ref, o_ref,
                   stage, stage4, q4, q16, k1, v1, k4, v4, k16, v16, pv_st, l_st, m_st):
    ti = pl.program_id(2)
    lane = lax.broadcasted_iota(jnp.int32, (1, LANES), 1)
    lo = lane < HEAD_DIM
    ones = jnp.ones((2 * BLOCK, LANES), bf16)

    for cur, halo, d1, d4, d16 in ((k_ref, kh_ref, k1, k4, k16), (v_ref, vh_ref, v1, v4, v16)):
        d1[0:BLOCK, :] = halo[0, B_TILE - BLOCK:B_TILE, :]
        d1[BLOCK:, :] = cur[0]
        for c in range(0, B_TILE, _CHUNK):
            stage[c:c + _CHUNK, :] = halo[0, c:c + _CHUNK, :].astype(f32)
            stage[B_TILE + c:B_TILE + c + _CHUNK, :] = cur[0, c:c + _CHUNK, :].astype(f32)
        _deinterleave(stage, stage4, 2 * B_TILE, (B_TILE - B_HALO4) // 4, d4, d16)
    for c in range(0, B_TILE, _CHUNK):
        stage[c:c + _CHUNK, :] = q_ref[0, c:c + _CHUNK, :].astype(f32)
    _deinterleave(stage, stage4, B_TILE, 0, q4, q16)

    def pair_block(br, q2, k2, v2, sel, rows):
        v2a = jnp.concatenate([v2, ones], axis=1)
        zero = jnp.zeros_like(q2)
        pv, l, m = [], [], []
        for e in range(2):
            qm = jnp.where(lo, q2, zero) if e == 0 else jnp.where(lo, zero, q2)
            s = lax.dot_general(qm, k2, _NT, preferred_element_type=f32)
            s = s + bias_ref[sel, br, e]
            me = jnp.max(s, axis=-1, keepdims=True)
            p = jnp.exp(s - me)
            pva = jnp.dot(p.astype(bf16), v2a, preferred_element_type=f32)
            pv.append(pva[:, :LANES])
            l.append(pva[:, LANES:])
            m.append(jnp.broadcast_to(me, (BLOCK, LANES)))
        pv_st[br, rows, :] = jnp.where(lo, pv[0], pv[1])
        l_st[br, rows, :] = jnp.where(lo, l[0], l[1])
        m_st[br, rows, :] = jnp.where(lo, m[0], m[1])

    def first_sel(cond):
        return jnp.where(cond, 1, 0)

    def branch1(j, c):
        r0 = pl.multiple_of(j * BLOCK, BLOCK)
        pair_block(0, q_ref[0, pl.ds(r0, BLOCK), :], k1[pl.ds(r0, 2 * BLOCK), :],
                   v1[pl.ds(r0, 2 * BLOCK), :],
                   first_sel(jnp.logical_and(ti == 0, j == 0)), pl.ds(r0, BLOCK))
        return c

    def branch4(i, c):
        r, j = i // 4, i % 4
        q0 = pl.multiple_of(r * (B_TILE // 4) + j * BLOCK, BLOCK)
        k0 = pl.multiple_of(r * B_ROWS4 + j * BLOCK, BLOCK)
        pair_block(1, q4[pl.ds(q0, BLOCK), :], k4[pl.ds(k0, 2 * BLOCK), :],
                   v4[pl.ds(k0, 2 * BLOCK), :],
                   first_sel(jnp.logical_and(ti == 0, j == 0)),
                   pl.ds(j * (4 * BLOCK) + r, BLOCK, stride=4))
        return c

    def branch16(r, c):
        q0 = pl.multiple_of(r * BLOCK, BLOCK)
        k0 = pl.multiple_of(r * B_ROWS16, BLOCK)
        pair_block(2, q16[pl.ds(q0, BLOCK), :], k16[pl.ds(k0, 2 * BLOCK), :],
                   v16[pl.ds(k0, 2 * BLOCK), :], first_sel(ti == 0),
                   pl.ds(r, BLOCK, stride=16))
        return c

    lax.fori_loop(0, B_NSUB, branch1, 0, unroll=16)
    lax.fori_loop(0, B_NSUB, branch4, 0, unroll=16)
    lax.fori_loop(0, B_NSUB, branch16, 0, unroll=16)

    def combine(c, carry):
        rows = pl.ds(pl.multiple_of(c * BLOCK, BLOCK), BLOCK)
        ms = [m_st[n, rows, :] for n in range(3)]
        mx = jnp.maximum(jnp.maximum(ms[0], ms[1]), ms[2])
        num = den = None
        for n in range(3):
            w = jnp.exp(ms[n] - mx)
            tn, td = w * pv_st[n, rows, :], w * l_st[n, rows, :]
            num = tn if num is None else num + tn
            den = td if den is None else den + td
        o_ref[0, rows, :] = num / den
        return carry

    lax.fori_loop(0, B_NSUB, combine, 0)


def _attention_b(q, k, v, bias):
    b, S, _ = q.shape
    assert S % B_TILE == 0
    npair = N_HEADS // 2
    cur = lambda bi, hp, ti: (bi, ti, hp)
    halo = lambda bi, hp, ti: (bi, jnp.maximum(ti - 1, 0), hp)
    blk = pl.BlockSpec((1, B_TILE, LANES), cur)
    hblk = pl.BlockSpec((1, B_TILE, LANES), halo)
    vm = lambda rows, dt: pltpu.VMEM((rows, LANES), dt)
    return pl.pallas_call(
        _attn_b_kernel,
        grid=(b, npair, S // B_TILE),
        in_specs=[blk, blk, hblk, blk, hblk,
                  pl.BlockSpec((2, 3, 2, BLOCK, 2 * BLOCK), lambda bi, hp, ti: (0, 0, hp, 0, 0))],
        out_specs=blk,
        out_shape=jax.ShapeDtypeStruct((b, S, Q_W), f32),
        scratch_shapes=[
            vm(2 * B_TILE, f32), vm(2 * B_TILE, f32),
            vm(B_TILE, bf16), vm(B_TILE, bf16),
            vm(B_TILE + BLOCK, bf16), vm(B_TILE + BLOCK, bf16),
            vm(4 * B_ROWS4, bf16), vm(4 * B_ROWS4, bf16),
            vm(16 * B_ROWS16, bf16), vm(16 * B_ROWS16, bf16),
            pltpu.VMEM((3, B_TILE, LANES), f32),
            pltpu.VMEM((3, B_TILE, LANES), f32),
            pltpu.VMEM((3, B_TILE, LANES), f32),
        ],
        compiler_params=pltpu.CompilerParams(
            dimension_semantics=("parallel", "parallel", "arbitrary"),
            vmem_limit_bytes=VMEM_LIMIT),
        name="attn_b",
    )(q, k, k, v, v, bias)


def _outproj_kernel(x_ref, ya_ref, ob_ref, g_ref, w_ref, o_ref):
    ob = ob_ref[...]
    ms = jnp.mean(ob * ob, axis=-1, keepdims=True)
    yb = ((ob * lax.rsqrt(ms + EPS)) * g_ref[...]).astype(bf16)
    acc = jnp.dot(ya_ref[...], w_ref[0:Q_W, :], preferred_element_type=f32)
    acc = acc + jnp.dot(yb, w_ref[Q_W:, :], preferred_element_type=f32)
    o_ref[...] = x_ref[...] + acc


def _outproj(x2d, ya, ob, g_b, w_bf, *, tm=512):
    T, D = x2d.shape
    return pl.pallas_call(
        _outproj_kernel,
        grid=(T // tm,),
        in_specs=[
            pl.BlockSpec((tm, D), lambda i: (i, 0)),
            pl.BlockSpec((tm, Q_W), lambda i: (i, 0)),
            pl.BlockSpec((tm, Q_W), lambda i: (i, 0)),
            pl.BlockSpec((1, Q_W), lambda i: (0, 0)),
            pl.BlockSpec(w_bf.shape, lambda i: (0, 0), pipeline_mode=pl.Buffered(1)),
        ],
        out_specs=pl.BlockSpec((tm, D), lambda i: (i, 0)),
        out_shape=jax.ShapeDtypeStruct((T, D), f32),
        compiler_params=pltpu.CompilerParams(
            dimension_semantics=("parallel",), vmem_limit_bytes=VMEM_LIMIT),
        name="outproj",
    )(x2d, ya, ob, g_b, w_bf)


def _mlp_kernel(x_ref, g_ref, w1_ref, w2_ref, gf_ref, o_ref, h_scr, *, final_norm):
    f = pl.program_id(1)

    @pl.when(f == 0)
    def _():
        x = x_ref[...]
        ms = jnp.mean(x * x, axis=-1, keepdims=True)
        h_scr[...] = ((x * lax.rsqrt(ms + EPS)) * g_ref[...]).astype(bf16)
        o_ref[...] = x

    u = jnp.dot(h_scr[...], w1_ref[...], preferred_element_type=f32)
    u = jnp.maximum(u, 0.0)
    o_ref[...] += jnp.dot((u * u).astype(bf16), w2_ref[...], preferred_element_type=f32)

    if final_norm:
        @pl.when(f == pl.num_programs(1) - 1)
        def _():
            y = o_ref[...]
            ms = jnp.mean(y * y, axis=-1, keepdims=True)
            o_ref[...] = (y * lax.rsqrt(ms + EPS)) * gf_ref[...]


def _mlp(x2d, g, w1_bf, w2_bf, gf, *, final_norm, tm=1024, tf=512):
    T, D = x2d.shape
    d_ff = w1_bf.shape[1]
    tm = min(tm, T)
    assert T % tm == 0 and d_ff % tf == 0
    return pl.pallas_call(
        functools.partial(_mlp_kernel, final_norm=final_norm),
        grid=(T // tm, d_ff // tf),
        in_specs=[
            pl.BlockSpec((tm, D), lambda i, f: (i, 0)),
            pl.BlockSpec((1, D), lambda i, f: (0, 0)),
            pl.BlockSpec((D, tf), lambda i, f: (0, f)),
            pl.BlockSpec((tf, D), lambda i, f: (f, 0)),
            pl.BlockSpec((1, D), lambda i, f: (0, 0)),
        ],
        out_specs=pl.BlockSpec((tm, D), lambda i, f: (i, 0)),
        out_shape=jax.ShapeDtypeStruct((T, D), f32),
        scratch_shapes=[pltpu.VMEM((tm, D), bf16)],
        compiler_params=pltpu.CompilerParams(
            dimension_semantics=("parallel", "arbitrary"), vmem_limit_bytes=VMEM_LIMIT),
        name="mlp",
    )(x2d, g, w1_bf, w2_bf, gf)


def kernel(x, g_attn, w_in, b_in, sinks_a, g_out_a, g_out_b, w_out, g_mlp, w_1, w_2, g_final):
    b, S, D = x.shape
    depth = w_in.shape[0]
    T = b * S
    bias_a = jnp.asarray(_band_bias(WINDOW_A - 1, 1))
    bias_b = jnp.asarray(np.stack([_band_bias(w // d, d) for w, d in DILATED_BRANCHES], axis=1))

    x2d = x.reshape(T, D)
    for l in range(depth):
        qa, ka, va, qb, kb, vb = _inproj(x2d, g_attn[l][None], w_in[l].astype(bf16), b_in[l][None])
        r3 = lambda t: t.reshape(b, S, t.shape[-1])
        ya = _attention_a(r3(qa), r3(ka), r3(va), bias_a, sinks_a[l], g_out_a[l][None])
        ob = _attention_b(r3(qb), r3(kb), r3(vb), bias_b)
        x1 = _outproj(x2d, ya.reshape(T, Q_W), ob.reshape(T, Q_W), g_out_b[l][None],
                      w_out[l].astype(bf16))
        x2d = _mlp(x1, g_mlp[l][None], w_1[l].astype(bf16), w_2[l].astype(bf16), g_final[None],
                   final_norm=(l == depth - 1))
    return x2d.reshape(b, S, D)
```

```python
import functools

import numpy as np
import jax
import jax.numpy as jnp
from jax import lax
from jax.experimental import pallas as pl
from jax.experimental.pallas import tpu as pltpu

HEAD_DIM = 64
N_HEADS = 16
N_KV_A = 2
BLOCK = 128
WINDOW_A = 128
DILATED_BRANCHES = ((128, 1), (512, 4), (2048, 16))
EPS = 1e-5
NEG_INF = -1e30
LOG2E = 1.4426950408889634
Q_W = N_HEADS * HEAD_DIM
KV_A_W = N_KV_A * HEAD_DIM
LANES = 128
VMEM_LIMIT = 56 * 1024 * 1024

bf16 = jnp.bfloat16
f32 = jnp.float32
_NT = (((1,), (1,)), ((), ()))


def _alibi_slopes(n):
    return 2.0 ** (-8.0 * (np.arange(n) + 1) / n)


def _band_bias(max_steps, step_dist):
    qi = np.arange(BLOCK)[:, None]
    kj = np.arange(2 * BLOCK)[None, :]
    steps = qi + BLOCK - kj
    valid = (steps >= 0) & (steps <= max_steps)
    slopes = _alibi_slopes(N_HEADS).astype(np.float32)
    pen = slopes[:, None, None] * (step_dist * steps).astype(np.float32)[None]
    pen = pen.astype(np.float64) * LOG2E
    gen =np.where(valid[None], -pen, np.float32(NEG_INF)).astype(np.float32)
    first = np.where((valid & (kj >= BLOCK))[None], -pen, np.float32(NEG_INF)).astype(np.float32)
    return np.stack([gen, first])


def _inproj_kernel(x_ref, g_ref, w_ref, b_ref, qa_ref, ka_ref, va_ref,
                   q1_ref, q4_ref, q16_ref, k1_ref, k4_ref, k16_ref, v1_ref, v4_ref, v16_ref,
                   ys, ys4, *, chunk):
    tm = x_ref.shape[0]
    x = x_ref[...]
    ms = jnp.mean(x * x, axis=-1, keepdims=True)
    h = ((x * lax.rsqrt(ms + EPS)) * g_ref[...]).astype(bf16)
    qscale = HEAD_DIM ** -0.5 * LOG2E

    def project(c0, ch, scale):
        y = jnp.dot(h, w_ref[:, c0:c0 + ch], preferred_element_type=f32) + b_ref[:, c0:c0 + ch]
        return y if scale is None else y * scale

    for cc in range(0, Q_W, chunk):
        qa_ref[:, cc:cc + chunk] = project(cc, chunk, qscale).astype(bf16)
    kva = project(Q_W, 2 * KV_A_W, None).astype(bf16)
    ka_ref[...] = kva[:, :KV_A_W]
    va_ref[...] = kva[:, KV_A_W:]
    c0 = Q_W + 2 * KV_A_W

    n4, n16 = tm // 4, tm // 16
    slot = 0
    for (r1, r4, r16), scale in (((q1_ref, q4_ref, q16_ref), qscale),
                                 ((k1_ref, k4_ref, k16_ref), None),
                                 ((v1_ref, v4_ref, v16_ref), None)):
        for cc in range(0, Q_W, chunk):
            cols = slice(cc, cc + chunk)
            y = project(c0 + cc, chunk, scale)
            r1[:, cols] = y.astype(bf16)
            for s in range(chunk // LANES):
                lanes = slice(cc + s * LANES, cc + (s + 1) * LANES)
                ys[slot, s] = y[:, s * LANES:(s + 1) * LANES]
                for r in range(4):
                    t = ys[slot, s, pl.ds(r, n4, stride=4), :]
                    r4[0, r, :, lanes] = t.astype(bf16)
                    ys4[slot, s, r * n4:(r + 1) * n4, :] = t
                for r in range(4):
                    for c in range(4):
                        r16[0, r + 4 * c, :, lanes] = ys4[
                            slot, s, pl.ds(r * n4 + c, n16, stride=4), :].astype(bf16)
            slot = 1 - slot
        c0 += Q_W


def _inproj(x2d, g, w_bf, b_in, batch, *, tm=512, chunk=512):
    T, D = x2d.shape
    S = T // batch
    d_in = w_bf.shape[1]
    assert 2 * KV_A_W + 4 * Q_W == d_in and S % tm == 0 and Q_W % chunk == 0
    tiles = S // tm
    const = lambda i: (0, 0)
    row = lambda i: (i, 0)
    res = lambda i: (i // tiles, 0, i % tiles, 0)
    a_widths = (Q_W, KV_A_W, KV_A_W)
    out_specs = [pl.BlockSpec((tm, w), row) for w in a_widths]
    out_shape = [jax.ShapeDtypeStruct((T, w), bf16) for w in a_widths]
    for _ in range(3):
        out_specs += [pl.BlockSpec((tm, Q_W), row),
                      pl.BlockSpec((1, 4, tm // 4, Q_W), res),
                      pl.BlockSpec((1, 16, tm // 16, Q_W), res)]
        out_shape += [jax.ShapeDtypeStruct((T, Q_W), bf16),
                      jax.ShapeDtypeStruct((batch, 4, S // 4, Q_W), bf16),
                      jax.ShapeDtypeStruct((batch, 16, S // 16, Q_W), bf16)]
    return pl.pallas_call(
        functools.partial(_inproj_kernel, chunk=chunk),
        grid=(T // tm,),
        in_specs=[
            pl.BlockSpec((tm, D), row),
            pl.BlockSpec((1, D), const),
            pl.BlockSpec((D, d_in), const, pipeline_mode=pl.Buffered(1)),
            pl.BlockSpec((1, d_in), const),
        ],
        out_specs=out_specs,
        out_shape=out_shape,
        scratch_shapes=[pltpu.VMEM((2, chunk // LANES, tm, LANES), f32),
                        pltpu.VMEM((2, chunk // LANES, tm, LANES), f32)],
        compiler_params=pltpu.CompilerParams(
            dimension_semantics=("parallel",), vmem_limit_bytes=VMEM_LIMIT),
        name="inproj",
    )(x2d, g, w_bf, b_in)


def _attn_a_kernel(q_ref, k_ref, kp_ref, v_ref, vp_ref, bias_ref, sink_ref, g_ref, out_ref,
                   k_scr, v_scr, o_scr, *, tq):
    li = pl.program_id(1)
    lane = lax.broadcasted_iota(jnp.int32, (1, LANES), 1)
    lo = lane < HEAD_DIM

    def dup(t):
        r = pltpu.roll(t.astype(f32), HEAD_DIM, 1).astype(bf16)
        return jnp.concatenate([jnp.where(lo, t, r), jnp.where(lo, r, t)], axis=1)
    k_scr[0:BLOCK, :] = dup(kp_ref[0])
    k_scr[BLOCK:, :] = dup(k_ref[0])
    v_scr[0:BLOCK, :] = dup(vp_ref[0])
    v_scr[BLOCK:, :] = dup(v_ref[0])

    def sub_block(j, carry):
        r0 = pl.multiple_of(j * BLOCK, BLOCK)
        rows = pl.ds(r0, BLOCK)
        krows = pl.ds(r0, 2 * BLOCK)
        sel = jnp.where(jnp.logical_and(li == 0, j == 0), 1, 0)
        for hp in range(N_HEADS // 2):
            cols = slice(hp * LANES, (hp + 1) * LANES)
            g = (2 * hp) // (N_HEADS // N_KV_A)
            kcols = slice(g * LANES, (g + 1) * LANES)
            q2 = q_ref[0, rows, cols]
            k2 = k_scr[krows, kcols]
            v2 = v_scr[krows, kcols]
            zero = jnp.zeros_like(q2)
            o = []
            for e in range(2):
                h = 2 * hp + e
                qm = jnp.where(lo, q2, zero) if e == 0 else jnp.where(lo, zero, q2)
                s = lax.dot_general(qm, k2, _NT, preferred_element_type=f32)
                s = s + bias_ref[sel, h]
                sink = sink_ref[h] * LOG2E
                m = jnp.maximum(jnp.max(s, axis=-1, keepdims=True), sink)
                p = jnp.exp2(s - m)
                l = jnp.sum(p, axis=-1, keepdims=True) + jnp.exp2(sink - m)
                pv = jnp.dot(p.astype(bf16), v2, preferred_element_type=f32)
                o.append(pv * (1.0 / l))
            o_scr[rows, cols] = jnp.where(lo, o[0], o[1])
        return carry

    lax.fori_loop(0, tq // BLOCK, sub_block, 0, unroll=2)

    o = o_scr[...]
    ms = jnp.mean(o * o, axis=-1, keepdims=True)
    out_ref[0] = ((o * lax.rsqrt(ms + EPS)) * g_ref[...]).astype(bf16)


def _attention_a(q, k, v, bias, sinks, gain, *, tq=512):
    b, S, _ = q.shape
    tq = min(tq, S)
    assert S % tq == 0 and tq % BLOCK == 0
    nsub = tq // BLOCK
    cur = lambda bi, li: (bi, li, 0)
    prv = lambda bi, li: (bi, jnp.maximum(li * nsub - 1, 0), 0)
    return pl.pallas_call(
        functools.partial(_attn_a_kernel, tq=tq),
        grid=(b, S // tq),
        in_specs=[
            pl.BlockSpec((1, tq, Q_W), cur),
            pl.BlockSpec((1, tq, KV_A_W), cur),
            pl.BlockSpec((1, BLOCK, KV_A_W), prv),
            pl.BlockSpec((1, tq, KV_A_W), cur),
            pl.BlockSpec((1, BLOCK, KV_A_W), prv),
            pl.BlockSpec(bias.shape, lambda bi, li: (0, 0, 0, 0)),
            pl.BlockSpec(memory_space=pltpu.SMEM),
            pl.BlockSpec((1, Q_W), lambda bi, li: (0, 0)),
        ],
        out_specs=pl.BlockSpec((1, tq, Q_W), cur),
        out_shape=jax.ShapeDtypeStruct((b, S, Q_W), bf16),
        scratch_shapes=[pltpu.VMEM((tq + BLOCK, 2 * KV_A_W), bf16),
                        pltpu.VMEM((tq + BLOCK, 2 * KV_A_W), bf16),
                        pltpu.VMEM((tq, Q_W), f32)],
        compiler_params=pltpu.CompilerParams(
            dimension_semantics=("parallel", "arbitrary"), vmem_limit_bytes=VMEM_LIMIT),
        name="attn_a",
    )(q, k, k, v, v, bias, sinks, gain)


B_TILE = 2048
B_DILS = tuple(d for _, d in DILATED_BRANCHES)
assert B_DILS == (1, 4, 16) and all(w // d == BLOCK for w, d in DILATED_BRANCHES)
B_NSUB = B_TILE // BLOCK


def _attn_b_kernel(q1, q4, q16, k1, k1h, k4, k4h, k16, k16h, v1, v1h, v4, v4h, v16, v16h,
                   bias_ref, o_ref, pv_st, l_st, m_st):
    first = jnp.where(pl.program_id(2) == 0, 1, 0)
    lane = lax.broadcasted_iota(jnp.int32, (1, LANES), 1)
    lo = lane < HEAD_DIM
    ones = jnp.ones((2 * BLOCK, LANES), bf16)

    def pair_block(br, q2, k2, v2, sel):
        v2a = jnp.concatenate([v2, ones], axis=1)
        zero = jnp.zeros_like(q2)
        pv, l, m = [], [], []
        for e in range(2):
            qm = jnp.where(lo, q2, zero) if e == 0 else jnp.where(lo, zero, q2)
            s = lax.dot_general(qm, k2, _NT, preferred_element_type=f32)
            s = s + bias_ref[sel, br, e]
            me = jnp.max(s, axis=-1, keepdims=True)
            p = jnp.exp2(s - me)
            pva = jnp.dot(p.astype(bf16), v2a, preferred_element_type=f32)
            pv.append(pva[:, :LANES])
            l.append(pva[:, LANES:])
            m.append(jnp.broadcast_to(me, (BLOCK, LANES)))
        return (jnp.where(lo, pv[0], pv[1]), jnp.where(lo, l[0], l[1]),
                jnp.where(lo, m[0], m[1]))

    def window(cur, halo, j):
        if j == 0:
            return jnp.concatenate([halo, cur[0:BLOCK, :]], axis=0)
        return cur[(j - 1) * BLOCK:(j + 1) * BLOCK, :]

    def keep(slot, rows, res):
        pv_st[slot, rows, :], l_st[slot, rows, :], m_st[slot, rows, :] = res

    for r in range(16):
        keep(1, pl.ds(r, BLOCK, stride=16),
             pair_block(2, q16[0, r], window(k16.at[0, r], k16h[0, r], 0),
                        window(v16.at[0, r], v16h[0, r], 0), first))
    for r in range(4):
        for j in range(B_NSUB // 4):
            keep(0, pl.ds(j * 4 * BLOCK + r, BLOCK, stride=4),
                 pair_block(1, q4[0, r, j * BLOCK:(j + 1) * BLOCK, :],
                            window(k4.at[0, r], k4h[0, r], j), window(v4.at[0, r], v4h[0, r], j),
                            first if j == 0 else 0))
    for j in range(B_NSUB):
        rows = slice(j * BLOCK, (j + 1) * BLOCK)
        pvs, ls, ms = pair_block(0, q1[0, rows, :], window(k1.at[0], k1h[0], j),
                                 window(v1.at[0], v1h[0], j), first if j == 0 else 0)
        pvs, ls, ms = [pvs], [ls], [ms]
        for slot in range(2):
            pvs.append(pv_st[slot, rows, :])
            ls.append(l_st[slot, rows, :])
            ms.append(m_st[slot, rows, :])
        mx = jnp.maximum(jnp.maximum(ms[0], ms[1]), ms[2])
        num = den = None
        for n in range(3):
            w = jnp.exp2(ms[n] - mx)
            num = w * pvs[n] if num is None else num + w * pvs[n]
            den = w * ls[n] if den is None else den + w * ls[n]
        o_ref[0, rows, :] = num / den


def _attention_b(qkv, bias, batch):
    q1, q4, q16, k1, k4, k16, v1, v4, v16 = qkv
    S = q1.shape[0] // batch
    assert S % B_TILE == 0
    r3 = lambda t: t.reshape(batch, S, Q_W)
    npair = N_HEADS // 2
    n4, n16 = B_TILE // 4, B_TILE // 16
    assert n16 == BLOCK
    nat = pl.BlockSpec((1, B_TILE, LANES), lambda bi, hp, ti: (bi, ti, hp))
    nat_h = pl.BlockSpec((1, BLOCK, LANES),
                         lambda bi, hp, ti: (bi, jnp.maximum(ti * B_NSUB - 1, 0), hp))
    d4 = pl.BlockSpec((1, 4, n4, LANES), lambda bi, hp, ti: (bi, 0, ti, hp))
    d4_h = pl.BlockSpec((1, 4, BLOCK, LANES),
                        lambda bi, hp, ti: (bi, 0, jnp.maximum(ti * (n4 // BLOCK) - 1, 0), hp))
    d16 = pl.BlockSpec((1, 16, n16, LANES), lambda bi, hp, ti: (bi, 0, ti, hp))
    d16_h = pl.BlockSpec((1, 16, BLOCK, LANES),
                         lambda bi, hp, ti: (bi, 0, jnp.maximum(ti - 1, 0), hp))
    return pl.pallas_call(
        _attn_b_kernel,
        grid=(batch, npair, S // B_TILE),
        in_specs=[nat, d4, d16,
                  nat, nat_h, d4, d4_h, d16, d16_h,
                  nat, nat_h, d4, d4_h, d16, d16_h,
                  pl.BlockSpec((2, 3, 2, BLOCK, 2 * BLOCK), lambda bi, hp, ti: (0, 0, hp, 0, 0))],
        out_specs=nat,
        out_shape=jax.ShapeDtypeStruct((batch, S, Q_W), f32),
        scratch_shapes=[pltpu.VMEM((2, B_TILE, LANES), f32),
                        pltpu.VMEM((2, B_TILE, LANES), f32),
                        pltpu.VMEM((2, B_TILE, LANES), f32)],
        compiler_params=pltpu.CompilerParams(
            dimension_semantics=("parallel", "parallel", "arbitrary"),
            vmem_limit_bytes=VMEM_LIMIT),
        name="attn_b",
    )(r3(q1), q4, q16, r3(k1), r3(k1), k4, k4, k16, k16, r3(v1), r3(v1), v4, v4, v16, v16, bias)


def _outproj_kernel(x_ref, ya_ref, ob_ref, g_ref, w_ref, o_ref):
    ob = ob_ref[...]
    ms = jnp.mean(ob * ob, axis=-1, keepdims=True)
    yb = ((ob * lax.rsqrt(ms + EPS)) * g_ref[...]).astype(bf16)
    acc = jnp.dot(ya_ref[...], w_ref[0:Q_W, :], preferred_element_type=f32)
    acc = acc + jnp.dot(yb, w_ref[Q_W:, :], preferred_element_type=f32)
    o_ref[...] = x_ref[...] + acc


def _outproj(x2d, ya, ob, g_b, w_bf, *, tm=512):
    T, D = x2d.shape
    return pl.pallas_call(
        _outproj_kernel,
        grid=(T // tm,),
        in_specs=[
            pl.BlockSpec((tm, D), lambda i: (i, 0)),
            pl.BlockSpec((tm, Q_W), lambda i: (i, 0)),
            pl.BlockSpec((tm, Q_W), lambda i: (i, 0)),
            pl.BlockSpec((1, Q_W), lambda i: (0, 0)),
            pl.BlockSpec(w_bf.shape, lambda i: (0, 0), pipeline_mode=pl.Buffered(1)),
        ],
        out_specs=pl.BlockSpec((tm, D), lambda i: (i, 0)),
        out_shape=jax.ShapeDtypeStruct((T, D), f32),
        compiler_params=pltpu.CompilerParams(
            dimension_semantics=("parallel",), vmem_limit_bytes=VMEM_LIMIT),
        name="outproj",
    )(x2d, ya, ob, g_b, w_bf)


def _mlp_kernel(x_ref, g_ref, w1_ref, w2_ref, gf_ref, o_ref, h_scr, *, final_norm):
    f = pl.program_id(1)

    @pl.when(f == 0)
    def _():
        x = x_ref[...]
        ms = jnp.mean(x * x, axis=-1, keepdims=True)
        h_scr[...] = ((x * lax.rsqrt(ms + EPS)) * g_ref[...]).astype(bf16)
        o_ref[...] = x

    u = jnp.dot(h_scr[...], w1_ref[...], preferred_element_type=f32)
    u = jnp.maximum(u, 0.0)
    o_ref[...] += jnp.dot((u * u).astype(bf16), w2_ref[...], preferred_element_type=f32)

    if final_norm:
        @pl.when(f == pl.num_programs(1) - 1)
        def _():
            y = o_ref[...]
            ms = jnp.mean(y * y, axis=-1, keepdims=True)
            o_ref[...] = (y * lax.rsqrt(ms + EPS)) * gf_ref[...]


def _mlp(x2d, g, w1_bf, w2_bf, gf, *, final_norm, tm=1024, tf=512):
    T, D = x2d.shape
    d_ff = w1_bf.shape[1]
    tm = min(tm, T)
    assert T % tm == 0 and d_ff % tf == 0
    return pl.pallas_call(
        functools.partial(_mlp_kernel, final_norm=final_norm),
        grid=(T // tm, d_ff // tf),
        in_specs=[
            pl.BlockSpec((tm, D), lambda i, f: (i, 0)),
            pl.BlockSpec((1, D), lambda i, f: (0, 0)),
            pl.BlockSpec((D, tf), lambda i, f: (0, f)),
            pl.BlockSpec((tf, D), lambda i, f: (f, 0)),
            pl.BlockSpec((1, D), lambda i, f: (0, 0)),
        ],
        out_specs=pl.BlockSpec((tm, D), lambda i, f: (i, 0)),
        out_shape=jax.ShapeDtypeStruct((T, D), f32),
        scratch_shapes=[pltpu.VMEM((tm, D), bf16)],
        compiler_params=pltpu.CompilerParams(
            dimension_semantics=("parallel", "arbitrary"), vmem_limit_bytes=VMEM_LIMIT),
        name="mlp",
    )(x2d, g, w1_bf, w2_bf, gf)


def kernel(x, g_attn, w_in, b_in, sinks_a, g_out_a, g_out_b, w_out, g_mlp, w_1, w_2, g_final):
    b, S, D = x.shape
    depth = w_in.shape[0]
    T = b * S
    bias_a = jnp.asarray(_band_bias(WINDOW_A - 1, 1))
    bias_b = jnp.asarray(np.stack([_band_bias(w // d, d) for w, d in DILATED_BRANCHES], axis=1))

    x2d = x.reshape(T, D)
    for l in range(depth):
        qa, ka, va, *qkv_b = _inproj(x2d, g_attn[l][None], w_in[l].astype(bf16), b_in[l][None], b)
        r3 = lambda t: t.reshape(b, S, t.shape[-1])
        ya = _attention_a(r3(qa), r3(ka), r3(va), bias_a, sinks_a[l], g_out_a[l][None])
        ob = _attention_b(qkv_b, bias_b, b)
        x1 = _outproj(x2d, ya.reshape(T, Q_W), ob.reshape(T, Q_W), g_out_b[l][None],
                      w_out[l].astype(bf16))
        x2d = _mlp(x1, g_mlp[l][None], w_1[l].astype(bf16), w_2[l].astype(bf16), g_final[None],
                   final_norm=(l == depth - 1))
    return x2d.reshape(b, S, D)
```

```python
import functools

import numpy as np
import jax
import jax.numpy as jnp
from jax import lax
from jax.experimental import pallas as pl
from jax.experimental.pallas import tpu as pltpu

HEAD_DIM = 64
N_HEADS = 16
N_KV_A = 2
BLOCK = 128
WINDOW_A = 128
DILATED_BRANCHES = ((128, 1), (512, 4), (2048, 16))
EPS = 1e-5
NEG_INF = -1e30
LOG2E = 1.4426950408889634
Q_W = N_HEADS * HEAD_DIM
KV_A_W = N_KV_A * HEAD_DIM
LANES = 128
VMEM_LIMIT = 56 * 1024 * 1024

bf16 = jnp.bfloat16
f32 = jnp.float32
_NT = (((1,), (1,)), ((), ()))


def _alibi_slopes(n):
    return 2.0 ** (-8.0 * (np.arange(n) + 1) / n)


def _band_bias(max_steps, step_dist):
    qi = np.arange(BLOCK)[:, None]
    kj = np.arange(2 * BLOCK)[None, :]
    steps = qi + BLOCK - kj
    valid = (steps >= 0) & (steps <= max_steps)
    slopes = _alibi_slopes(N_HEADS).astype(np.float32)
    pen = slopes[:, None, None] * (step_dist * steps).astype(np.float32)[None]
    pen = pen.astype(np.float64) * LOG2E
    gen =np.where(valid[None], -pen, np.float32(NEG_INF)).astype(np.float32)
    first = np.where((valid & (kj >= BLOCK))[None], -pen, np.float32(NEG_INF)).astype(np.float32)
    return np.stack([gen, first])


def _inproj_kernel(x_ref, g_ref, w_ref, b_ref, qa_ref, kva_ref, b1_ref, b4_ref, b16_ref,
                   ys, ys4, *, chunk):
    tm = x_ref.shape[0]
    x = x_ref[...]
    ms = jnp.mean(x * x, axis=-1, keepdims=True)
    h = ((x * lax.rsqrt(ms + EPS)) * g_ref[...]).astype(bf16)
    qscale = HEAD_DIM ** -0.5 * LOG2E

    def project(c0, ch, scale):
        y = jnp.dot(h, w_ref[:, c0:c0 + ch], preferred_element_type=f32) + b_ref[:, c0:c0 + ch]
        return y if scale is None else y * scale

    for cc in range(0, Q_W, chunk):
        qa_ref[:, cc:cc + chunk] = project(cc, chunk, qscale).astype(bf16)
    kva_ref[...] = project(Q_W, 2 * KV_A_W, None).astype(bf16)
    c0 = Q_W + 2 * KV_A_W

    n4, n16 = tm // 4, tm // 16
    slot = 0
    for cc in range(0, 3 * Q_W, chunk):
        y = project(c0 + cc, chunk, qscale if cc < Q_W else None)
        b1_ref[:, cc:cc + chunk] = y.astype(bf16)
        for s in range(chunk // LANES):
            lanes = slice(cc + s * LANES, cc + (s + 1) * LANES)
            ys[slot, s] = y[:, s * LANES:(s + 1) * LANES]
            for r in range(4):
                t = ys[slot, s, pl.ds(r, n4, stride=4), :]
                b4_ref[0, r, :, lanes] = t.astype(bf16)
                ys4[slot, s, r * n4:(r + 1) * n4, :] = t
            for r in range(4):
                for c in range(4):
                    b16_ref[0, r + 4 * c, :, lanes] = ys4[
                        slot, s, pl.ds(r * n4 + c, n16, stride=4), :].astype(bf16)
        slot = 1 - slot


def _inproj(x2d, g, w_bf, b_in, batch, *, tm=512, chunk=512):
    T, D = x2d.shape
    S = T // batch
    d_in = w_bf.shape[1]
    assert 2 * KV_A_W + 4 * Q_W == d_in and S % tm == 0 and Q_W % chunk == 0
    tiles = S // tm
    const = lambda i: (0, 0)
    row = lambda i: (i, 0)
    res = lambda i: (i // tiles, 0, i % tiles, 0)
    return pl.pallas_call(
        functools.partial(_inproj_kernel, chunk=chunk),
        grid=(T // tm,),
        in_specs=[
            pl.BlockSpec((tm, D), row),
            pl.BlockSpec((1, D), const),
            pl.BlockSpec((D, d_in), const, pipeline_mode=pl.Buffered(1)),
            pl.BlockSpec((1, d_in), const),
        ],
        out_specs=[pl.BlockSpec((tm, Q_W), row),
                   pl.BlockSpec((tm, 2 * KV_A_W), row),
                   pl.BlockSpec((tm, 3 * Q_W), row),
                   pl.BlockSpec((1, 4, tm // 4, 3 * Q_W), res),
                   pl.BlockSpec((1, 16, tm // 16, 3 * Q_W), res)],
        out_shape=[jax.ShapeDtypeStruct((T, Q_W), bf16),
                   jax.ShapeDtypeStruct((T, 2 * KV_A_W), bf16),
                   jax.ShapeDtypeStruct((T, 3 * Q_W), bf16),
                   jax.ShapeDtypeStruct((batch, 4, S // 4, 3 * Q_W), bf16),
                   jax.ShapeDtypeStruct((batch, 16, S // 16, 3 * Q_W), bf16)],
        scratch_shapes=[pltpu.VMEM((2, chunk // LANES, tm, LANES), f32),
                        pltpu.VMEM((2, chunk // LANES, tm, LANES), f32)],
        compiler_params=pltpu.CompilerParams(
            dimension_semantics=("parallel",), vmem_limit_bytes=VMEM_LIMIT),
        name="inproj",
    )(x2d, g, w_bf, b_in)


def _attn_a_kernel(q_ref, k_ref, kp_ref, v_ref, vp_ref, bias_ref, sink_ref, g_ref, out_ref,
                   k_scr, v_scr, o_scr, *, tq):
    li = pl.program_id(1)
    lane = lax.broadcasted_iota(jnp.int32, (1, LANES), 1)
    lo = lane < HEAD_DIM

    def dup(t):
        r = pltpu.roll(t.astype(f32), HEAD_DIM, 1).astype(bf16)
        return jnp.concatenate([jnp.where(lo, t, r), jnp.where(lo, r, t)], axis=1)
    k_scr[0:BLOCK, :] = dup(kp_ref[0])
    k_scr[BLOCK:, :] = dup(k_ref[0])
    v_scr[0:BLOCK, :] = dup(vp_ref[0])
    v_scr[BLOCK:, :] = dup(v_ref[0])

    first = jnp.where(li == 0, 1, 0)
    ones = jnp.ones((2 * BLOCK, LANES), bf16)
    for j in range(tq // BLOCK):
        rows = slice(j * BLOCK, (j + 1) * BLOCK)
        krows = slice(j * BLOCK, (j + 2) * BLOCK)
        sel = first if j == 0 else 0
        for hp in range(N_HEADS // 2):
            cols = slice(hp * LANES, (hp + 1) * LANES)
            g = (2 * hp) // (N_HEADS // N_KV_A)
            kcols = slice(g * LANES, (g + 1) * LANES)
            q2 = q_ref[0, rows, cols]
            k2 = k_scr[krows, kcols]
            v2a = jnp.concatenate([v_scr[krows, kcols], ones], axis=1)
            zero = jnp.zeros_like(q2)
            pv, l = [], []
            for e in range(2):
                h = 2 * hp + e
                qm = jnp.where(lo, q2, zero) if e == 0 else jnp.where(lo, zero, q2)
                s = lax.dot_general(qm, k2, _NT, preferred_element_type=f32)
                s = s + bias_ref[sel, h]
                sink = sink_ref[h] * LOG2E
                m = jnp.maximum(jnp.max(s, axis=-1, keepdims=True), sink)
                p = jnp.exp2(s - m)
                pva = jnp.dot(p.astype(bf16), v2a, preferred_element_type=f32)
                pv.append(pva[:, :LANES])
                l.append(pva[:, LANES:] + jnp.exp2(sink - m))
            o_scr[rows, cols] = jnp.where(lo, pv[0], pv[1]) / jnp.where(lo, l[0], l[1])

    o = o_scr[...]
    ms = jnp.mean(o * o, axis=-1, keepdims=True)
    out_ref[0] = ((o * lax.rsqrt(ms + EPS)) * g_ref[...]).astype(bf16)


def _attention_a(q, kv, bias, sinks, gain, *, tq=512):
    b, S, _ = q.shape
    tq = min(tq, S)
    assert S % tq == 0 and tq % BLOCK == 0
    nsub = tq // BLOCK
    cur = lambda bi, li: (bi, li, 0)
    k, v = kv, kv
    return pl.pallas_call(
        functools.partial(_attn_a_kernel, tq=tq),
        grid=(b, S // tq),
        in_specs=[
            pl.BlockSpec((1, tq, Q_W), cur),
            pl.BlockSpec((1, tq, KV_A_W), lambda bi, li: (bi, li, 0)),
            pl.BlockSpec((1, BLOCK, KV_A_W), lambda bi, li: (bi, jnp.maximum(li * nsub - 1, 0), 0)),
            pl.BlockSpec((1, tq, KV_A_W), lambda bi, li: (bi, li, 1)),
            pl.BlockSpec((1, BLOCK, KV_A_W), lambda bi, li: (bi, jnp.maximum(li * nsub - 1, 0), 1)),
            pl.BlockSpec(bias.shape, lambda bi, li: (0, 0, 0, 0)),
            pl.BlockSpec(memory_space=pltpu.SMEM),
            pl.BlockSpec((1, Q_W), lambda bi, li: (0, 0)),
        ],
        out_specs=pl.BlockSpec((1, tq, Q_W), cur),
        out_shape=jax.ShapeDtypeStruct((b, S, Q_W), bf16),
        scratch_shapes=[pltpu.VMEM((tq + BLOCK, 2 * KV_A_W), bf16),
                        pltpu.VMEM((tq + BLOCK, 2 * KV_A_W), bf16),
                        pltpu.VMEM((tq, Q_W), f32)],
        compiler_params=pltpu.CompilerParams(
            dimension_semantics=("parallel", "arbitrary"), vmem_limit_bytes=VMEM_LIMIT),
        name="attn_a",
    )(q, k, k, v, v, bias, sinks, gain)


B_TILE = 2048
B_DILS = tuple(d for _, d in DILATED_BRANCHES)
assert B_DILS == (1, 4, 16) and all(w // d == BLOCK for w, d in DILATED_BRANCHES)
B_NSUB = B_TILE // BLOCK


def _attn_b_kernel(q1, q4, q16, k1, k1h, k4, k4h, k16, k16h, v1, v1h, v4, v4h, v16, v16h,
                   bias_ref, o_ref, pv_st, l_st, m_st):
    first = jnp.where(pl.program_id(2) == 0, 1, 0)
    lane = lax.broadcasted_iota(jnp.int32, (1, LANES), 1)
    lo = lane < HEAD_DIM
    ones = jnp.ones((2 * BLOCK, LANES), bf16)

    def pair_block(br, q2, k2, v2, sel):
        v2a = jnp.concatenate([v2, ones], axis=1)
        zero = jnp.zeros_like(q2)
        pv, l, m = [], [], []
        for e in range(2):
            qm = jnp.where(lo, q2, zero) if e == 0 else jnp.where(lo, zero, q2)
            s = lax.dot_general(qm, k2, _NT, preferred_element_type=f32)
            s = s + bias_ref[sel, br, e]
            me = jnp.max(s, axis=-1, keepdims=True)
            p = jnp.exp2(s - me)
            pva = jnp.dot(p.astype(bf16), v2a, preferred_element_type=f32)
            pv.append(pva[:, :LANES])
            l.append(pva[:, LANES:])
            m.append(jnp.broadcast_to(me, (BLOCK, LANES)))
        return (jnp.where(lo, pv[0], pv[1]), jnp.where(lo, l[0], l[1]),
                jnp.where(lo, m[0], m[1]))

    def window(cur, halo, j):
        if j == 0:
            return jnp.concatenate([halo, cur[0:BLOCK, :]], axis=0)
        return cur[(j - 1) * BLOCK:(j + 1) * BLOCK, :]

    def keep(slot, rows, res):
        pv_st[slot, rows, :], l_st[slot, rows, :], m_st[slot, rows, :] = res

    for r in range(16):
        keep(1, pl.ds(r, BLOCK, stride=16),
             pair_block(2, q16[0, r], window(k16.at[0, r], k16h[0, r], 0),
                        window(v16.at[0, r], v16h[0, r], 0), first))
    for r in range(4):
        for j in range(B_NSUB // 4):
            keep(0, pl.ds(j * 4 * BLOCK + r, BLOCK, stride=4),
                 pair_block(1, q4[0, r, j * BLOCK:(j + 1) * BLOCK, :],
                            window(k4.at[0, r], k4h[0, r], j), window(v4.at[0, r], v4h[0, r], j),
                            first if j == 0 else 0))
    for j in range(B_NSUB):
        rows = slice(j * BLOCK, (j + 1) * BLOCK)
        pvs, ls, ms = pair_block(0, q1[0, rows, :], window(k1.at[0], k1h[0], j),
                                 window(v1.at[0], v1h[0], j), first if j == 0 else 0)
        pvs, ls, ms = [pvs], [ls], [ms]
        for slot in range(2):
            pvs.append(pv_st[slot, rows, :])
            ls.append(l_st[slot, rows, :])
            ms.append(m_st[slot, rows, :])
        mx = jnp.maximum(jnp.maximum(ms[0], ms[1]), ms[2])
        num = den = None
        for n in range(3):
            w = jnp.exp2(ms[n] - mx)
            num = w * pvs[n] if num is None else num + w * pvs[n]
            den = w * ls[n] if den is None else den + w * ls[n]
        o_ref[0, rows, :] = num / den


def _attention_b(qkv, bias, batch):
    b1, b4, b16 = qkv
    S = b1.shape[0] // batch
    assert S % B_TILE == 0
    b1 = b1.reshape(batch, S, 3 * Q_W)
    npair = N_HEADS // 2
    n4, n16 = B_TILE // 4, B_TILE // 16
    assert n16 == BLOCK

    def specs(part):
        col = lambda hp: part * npair + hp
        nat = pl.BlockSpec((1, B_TILE, LANES), lambda bi, hp, ti: (bi, ti, col(hp)))
        nat_h = pl.BlockSpec((1, BLOCK, LANES), lambda bi, hp, ti: (
            bi, jnp.maximum(ti * B_NSUB - 1, 0), col(hp)))
        d4 = pl.BlockSpec((1, 4, n4, LANES), lambda bi, hp, ti: (bi, 0, ti, col(hp)))
        d4_h = pl.BlockSpec((1, 4, BLOCK, LANES), lambda bi, hp, ti: (
            bi, 0, jnp.maximum(ti * (n4 // BLOCK) - 1, 0), col(hp)))
        d16 = pl.BlockSpec((1, 16, n16, LANES), lambda bi, hp, ti: (bi, 0, ti, col(hp)))
        d16_h = pl.BlockSpec((1, 16, BLOCK, LANES), lambda bi, hp, ti: (
            bi, 0, jnp.maximum(ti - 1, 0), col(hp)))
        return nat, nat_h, d4, d4_h, d16, d16_h

    q_nat, _, q_d4, _, q_d16, _ = specs(0)
    kv_args = (b1, b1, b4, b4, b16, b16)
    return pl.pallas_call(
        _attn_b_kernel,
        grid=(batch, npair, S // B_TILE),
        in_specs=[q_nat, q_d4, q_d16, *specs(1), *specs(2),
                  pl.BlockSpec((2, 3, 2, BLOCK, 2 * BLOCK), lambda bi, hp, ti: (0, 0, hp, 0, 0))],
        out_specs=pl.BlockSpec((1, B_TILE, LANES), lambda bi, hp, ti: (bi, ti, hp)),
        out_shape=jax.ShapeDtypeStruct((batch, S, Q_W), f32),
        scratch_shapes=[pltpu.VMEM((2, B_TILE, LANES), f32),
                        pltpu.VMEM((2, B_TILE, LANES), f32),
                        pltpu.VMEM((2, B_TILE, LANES), f32)],
        compiler_params=pltpu.CompilerParams(
            dimension_semantics=("parallel", "parallel", "arbitrary"),
            vmem_limit_bytes=VMEM_LIMIT),
        name="attn_b",
    )(b1, b4, b16, *kv_args, *kv_args, bias)


def _outproj_kernel(x_ref, ya_ref, ob_ref, g_ref, w_ref, o_ref):
    ob = ob_ref[...]
    ms = jnp.mean(ob * ob, axis=-1, keepdims=True)
    yb = ((ob * lax.rsqrt(ms + EPS)) * g_ref[...]).astype(bf16)
    acc = jnp.dot(ya_ref[...], w_ref[0:Q_W, :], preferred_element_type=f32)
    acc = acc + jnp.dot(yb, w_ref[Q_W:, :], preferred_element_type=f32)
    o_ref[...] = x_ref[...] + acc


def _outproj(x2d, ya, ob, g_b, w_bf, *, tm=512):
    T, D = x2d.shape
    return pl.pallas_call(
        _outproj_kernel,
        grid=(T // tm,),
        in_specs=[
            pl.BlockSpec((tm, D), lambda i: (i, 0)),
            pl.BlockSpec((tm, Q_W), lambda i: (i, 0)),
            pl.BlockSpec((tm, Q_W), lambda i: (i, 0)),
            pl.BlockSpec((1, Q_W), lambda i: (0, 0)),
            pl.BlockSpec(w_bf.shape, lambda i: (0, 0), pipeline_mode=pl.Buffered(1)),
        ],
        out_specs=pl.BlockSpec((tm, D), lambda i: (i, 0)),
        out_shape=jax.ShapeDtypeStruct((T, D), f32),
        compiler_params=pltpu.CompilerParams(
            dimension_semantics=("parallel",), vmem_limit_bytes=VMEM_LIMIT),
        name="outproj",
    )(x2d, ya, ob, g_b, w_bf)


def _mlp_kernel(x_ref, g_ref, w1_ref, w2_ref, gf_ref, o_ref, h_scr, *, final_norm):
    f = pl.program_id(1)

    @pl.when(f == 0)
    def _():
        x = x_ref[...]
        ms = jnp.mean(x * x, axis=-1, keepdims=True)
        h_scr[...] = ((x * lax.rsqrt(ms + EPS)) * g_ref[...]).astype(bf16)
        o_ref[...] = x

    u = jnp.dot(h_scr[...], w1_ref[...], preferred_element_type=f32)
    u = jnp.maximum(u, 0.0)
    o_ref[...] += jnp.dot((u * u).astype(bf16), w2_ref[...], preferred_element_type=f32)

    if final_norm:
        @pl.when(f == pl.num_programs(1) - 1)
        def _():
            y = o_ref[...]
            ms = jnp.mean(y * y, axis=-1, keepdims=True)
            o_ref[...] = (y * lax.rsqrt(ms + EPS)) * gf_ref[...]


def _mlp(x2d, g, w1_bf, w2_bf, gf, *, final_norm, tm=1024, tf=512):
    T, D = x2d.shape
    d_ff = w1_bf.shape[1]
    tm = min(tm, T)
    assert T % tm == 0 and d_ff % tf == 0
    return pl.pallas_call(
        functools.partial(_mlp_kernel, final_norm=final_norm),
        grid=(T // tm, d_ff // tf),
        in_specs=[
            pl.BlockSpec((tm, D), lambda i, f: (i, 0)),
            pl.BlockSpec((1, D), lambda i, f: (0, 0)),
            pl.BlockSpec((D, tf), lambda i, f: (0, f)),
            pl.BlockSpec((tf, D), lambda i, f: (f, 0)),
            pl.BlockSpec((1, D), lambda i, f: (0, 0)),
        ],
        out_specs=pl.BlockSpec((tm, D), lambda i, f: (i, 0)),
        out_shape=jax.ShapeDtypeStruct((T, D), f32),
        scratch_shapes=[pltpu.VMEM((tm, D), bf16)],
        compiler_params=pltpu.CompilerParams(
            dimension_semantics=("parallel", "arbitrary"), vmem_limit_bytes=VMEM_LIMIT),
        name="mlp",
    )(x2d, g, w1_bf, w2_bf, gf)


def kernel(x, g_attn, w_in, b_in, sinks_a, g_out_a, g_out_b, w_out, g_mlp, w_1, w_2, g_final):
    b, S, D = x.shape
    depth = w_in.shape[0]
    T = b * S
    bias_a = jnp.asarray(_band_bias(WINDOW_A - 1, 1))
    bias_b = jnp.asarray(np.stack([_band_bias(w // d, d) for w, d in DILATED_BRANCHES], axis=1))

    x2d = x.reshape(T, D)
    for l in range(depth):
        qa, kva, *qkv_b = _inproj(x2d, g_attn[l][None], w_in[l].astype(bf16), b_in[l][None], b)
        r3 = lambda t: t.reshape(b, S, t.shape[-1])
        ya = _attention_a(r3(qa), r3(kva), bias_a, sinks_a[l], g_out_a[l][None])
        ob = _attention_b(qkv_b, bias_b, b)
        x1 = _outproj(x2d, ya.reshape(T, Q_W), ob.reshape(T, Q_W), g_out_b[l][None],
                      w_out[l].astype(bf16))
        x2d = _mlp(x1, g_mlp[l][None], w_1[l].astype(bf16), w_2[l].astype(bf16), g_final[None],
                   final_norm=(l == depth - 1))
    return x2d.reshape(b, S, D)
```

```python
import functools

import numpy as np
import jax
import jax.numpy as jnp
from jax import lax
from jax.experimental import pallas as pl
from jax.experimental.pallas import tpu as pltpu

HEAD_DIM = 64
N_HEADS = 16
N_KV_A = 2
BLOCK = 128
WINDOW_A = 128
DILATED_BRANCHES = ((128, 1), (512, 4), (2048, 16))
EPS = 1e-5
NEG_INF = -1e30
LOG2E = 1.4426950408889634
Q_W = N_HEADS * HEAD_DIM
KV_A_W = N_KV_A * HEAD_DIM
LANES = 128
VMEM_LIMIT = 60 * 1024 * 1024

bf16 = jnp.bfloat16
f32 = jnp.float32
_NT = (((1,), (1,)), ((), ()))


def _alibi_slopes(n):
    return 2.0 ** (-8.0 * (np.arange(n) + 1) / n)


def _band_bias(max_steps, step_dist):
    qi = np.arange(BLOCK)[:, None]
    kj = np.arange(2 * BLOCK)[None, :]
    steps = qi + BLOCK - kj
    valid = (steps >= 0) & (steps <= max_steps)
    slopes = _alibi_slopes(N_HEADS).astype(np.float32)
    pen = slopes[:, None, None] * (step_dist * steps).astype(np.float32)[None]
    pen = pen.astype(np.float64) * LOG2E
    gen =np.where(valid[None], -pen, np.float32(NEG_INF)).astype(np.float32)
    first = np.where((valid & (kj >= BLOCK))[None], -pen, np.float32(NEG_INF)).astype(np.float32)
    return np.stack([gen, first])


def _inproj_kernel(x_ref, g_ref, w_ref, b_ref, qa_ref, kva_ref, b1_ref, b4_ref, b16_ref,
                   ys, ys4, *, chunk):
    tm = x_ref.shape[0]
    x = x_ref[...]
    ms = jnp.mean(x * x, axis=-1, keepdims=True)
    h = ((x * lax.rsqrt(ms + EPS)) * g_ref[...]).astype(bf16)
    qscale = HEAD_DIM ** -0.5 * LOG2E

    def project(c0, ch, scale):
        y = jnp.dot(h, w_ref[:, c0:c0 + ch], preferred_element_type=f32) + b_ref[:, c0:c0 + ch]
        return y if scale is None else y * scale

    for cc in range(0, Q_W, chunk):
        qa_ref[:, cc:cc + chunk] = project(cc, chunk, qscale).astype(bf16)
    kva_ref[...] = project(Q_W, 2 * KV_A_W, None).astype(bf16)
    c0 = Q_W + 2 * KV_A_W

    n4, n16 = tm // 4, tm // 16
    slot = 0
    for cc in range(0, 3 * Q_W, chunk):
        y = project(c0 + cc, chunk, qscale if cc < Q_W else None)
        b1_ref[:, cc:cc + chunk] = y.astype(bf16)
        for s in range(chunk // LANES):
            lanes = slice(cc + s * LANES, cc + (s + 1) * LANES)
            ys[slot, s] = y[:, s * LANES:(s + 1) * LANES]
            for r in range(4):
                t = ys[slot, s, pl.ds(r, n4, stride=4), :]
                b4_ref[0, r, :, lanes] = t.astype(bf16)
                ys4[slot, s, r * n4:(r + 1) * n4, :] = t
            for r in range(4):
                for c in range(4):
                    b16_ref[0, r + 4 * c, :, lanes] = ys4[
                        slot, s, pl.ds(r * n4 + c, n16, stride=4), :].astype(bf16)
        slot = 1 - slot


def _inproj(x2d, g, w_bf, b_in, batch, *, tm=512, chunk=512):
    T, D = x2d.shape
    S = T // batch
    d_in = w_bf.shape[1]
    assert 2 * KV_A_W + 4 * Q_W == d_in and S % tm == 0 and Q_W % chunk == 0
    tiles = S // tm
    const = lambda i: (0, 0)
    row = lambda i: (i, 0)
    res = lambda i: (i // tiles, 0, i % tiles, 0)
    return pl.pallas_call(
        functools.partial(_inproj_kernel, chunk=chunk),
        grid=(T // tm,),
        in_specs=[
            pl.BlockSpec((tm, D), row),
            pl.BlockSpec((1, D), const),
            pl.BlockSpec((D, d_in), const, pipeline_mode=pl.Buffered(1)),
            pl.BlockSpec((1, d_in), const),
        ],
        out_specs=[pl.BlockSpec((tm, Q_W), row),
                   pl.BlockSpec((tm, 2 * KV_A_W), row),
                   pl.BlockSpec((tm, 3 * Q_W), row),
                   pl.BlockSpec((1, 4, tm // 4, 3 * Q_W), res),
                   pl.BlockSpec((1, 16, tm // 16, 3 * Q_W), res)],
        out_shape=[jax.ShapeDtypeStruct((T, Q_W), bf16),
                   jax.ShapeDtypeStruct((T, 2 * KV_A_W), bf16),
                   jax.ShapeDtypeStruct((T, 3 * Q_W), bf16),
                   jax.ShapeDtypeStruct((batch, 4, S // 4, 3 * Q_W), bf16),
                   jax.ShapeDtypeStruct((batch, 16, S // 16, 3 * Q_W), bf16)],
        scratch_shapes=[pltpu.VMEM((2, chunk // LANES, tm, LANES), f32),
                        pltpu.VMEM((2, chunk // LANES, tm, LANES), f32)],
        compiler_params=pltpu.CompilerParams(
            dimension_semantics=("parallel",), vmem_limit_bytes=VMEM_LIMIT),
        name="inproj",
    )(x2d, g, w_bf, b_in)


def _attn_a_kernel(q_ref, k_ref, kp_ref, v_ref, vp_ref, bias_ref, sink_ref, g_ref, out_ref,
                   k_scr, v_scr, o_scr, *, tq):
    li = pl.program_id(1)
    lane = lax.broadcasted_iota(jnp.int32, (1, LANES), 1)
    lo = lane < HEAD_DIM

    def dup(t):
        r = pltpu.roll(t.astype(f32), HEAD_DIM, 1).astype(bf16)
        return jnp.concatenate([jnp.where(lo, t, r), jnp.where(lo, r, t)], axis=1)
    k_scr[0:BLOCK, :] = dup(kp_ref[0])
    k_scr[BLOCK:, :] = dup(k_ref[0])
    v_scr[0:BLOCK, :] = dup(vp_ref[0])
    v_scr[BLOCK:, :] = dup(v_ref[0])

    first = jnp.where(li == 0, 1, 0)
    ones = jnp.ones((2 * BLOCK, LANES), bf16)
    for j in range(tq // BLOCK):
        rows = slice(j * BLOCK, (j + 1) * BLOCK)
        krows = slice(j * BLOCK, (j + 2) * BLOCK)
        sel = first if j == 0 else 0
        for hp in range(N_HEADS // 2):
            cols = slice(hp * LANES, (hp + 1) * LANES)
            g = (2 * hp) // (N_HEADS // N_KV_A)
            kcols = slice(g * LANES, (g + 1) * LANES)
            q2 = q_ref[0, rows, cols]
            k2 = k_scr[krows, kcols]
            v2a = jnp.concatenate([v_scr[krows, kcols], ones], axis=1)
            zero = jnp.zeros_like(q2)
            pv, l = [], []
            for e in range(2):
                h = 2 * hp + e
                qm = jnp.where(lo, q2, zero) if e == 0 else jnp.where(lo, zero, q2)
                s = lax.dot_general(qm, k2, _NT, preferred_element_type=f32)
                s = s + bias_ref[sel, h]
                sink = sink_ref[h] * LOG2E
                m = jnp.maximum(jnp.max(s, axis=-1, keepdims=True), sink)
                p = jnp.exp2(s - m)
                pva = jnp.dot(p.astype(bf16), v2a, preferred_element_type=f32)
                pv.append(pva[:, :LANES])
                l.append(pva[:, LANES:] + jnp.exp2(sink - m))
            o_scr[rows, cols] = jnp.where(lo, pv[0], pv[1]) / jnp.where(lo, l[0], l[1])

    o = o_scr[...]
    ms = jnp.mean(o * o, axis=-1, keepdims=True)
    out_ref[0] = ((o * lax.rsqrt(ms + EPS)) * g_ref[...]).astype(bf16)


def _attention_a(q, kv, bias, sinks, gain, *, tq=512):
    b, S, _ = q.shape
    tq = min(tq, S)
    assert S % tq == 0 and tq % BLOCK == 0
    nsub = tq // BLOCK
    cur = lambda bi, li: (bi, li, 0)
    k, v = kv, kv
    return pl.pallas_call(
        functools.partial(_attn_a_kernel, tq=tq),
        grid=(b, S // tq),
        in_specs=[
            pl.BlockSpec((1, tq, Q_W), cur),
            pl.BlockSpec((1, tq, KV_A_W), lambda bi, li: (bi, li, 0)),
            pl.BlockSpec((1, BLOCK, KV_A_W), lambda bi, li: (bi, jnp.maximum(li * nsub - 1, 0), 0)),
            pl.BlockSpec((1, tq, KV_A_W), lambda bi, li: (bi, li, 1)),
            pl.BlockSpec((1, BLOCK, KV_A_W), lambda bi, li: (bi, jnp.maximum(li * nsub - 1, 0), 1)),
            pl.BlockSpec(bias.shape, lambda bi, li: (0, 0, 0, 0)),
            pl.BlockSpec(memory_space=pltpu.SMEM),
            pl.BlockSpec((1, Q_W), lambda bi, li: (0, 0)),
        ],
        out_specs=pl.BlockSpec((1, tq, Q_W), cur),
        out_shape=jax.ShapeDtypeStruct((b, S, Q_W), bf16),
        scratch_shapes=[pltpu.VMEM((tq + BLOCK, 2 * KV_A_W), bf16),
                        pltpu.VMEM((tq + BLOCK, 2 * KV_A_W), bf16),
                        pltpu.VMEM((tq, Q_W), f32)],
        compiler_params=pltpu.CompilerParams(
            dimension_semantics=("parallel", "arbitrary"), vmem_limit_bytes=VMEM_LIMIT),
        name="attn_a",
    )(q, k, k, v, v, bias, sinks, gain)


B_TILE = 2048
B_DILS = tuple(d for _, d in DILATED_BRANCHES)
assert B_DILS == (1, 4, 16) and all(w // d == BLOCK for w, d in DILATED_BRANCHES)
B_NSUB = B_TILE // BLOCK


def _attn_b_kernel(q1, q4, q16, k1, k1h, k4, k4h, k16, k16h, v1, v1h, v4, v4h, v16, v16h,
                   bias_ref, o_ref, pv_st, l_st, m_st):
    first = jnp.where(pl.program_id(2) == 0, 1, 0)
    lane = lax.broadcasted_iota(jnp.int32, (1, LANES), 1)
    lo = lane < HEAD_DIM
    ones = jnp.ones((2 * BLOCK, LANES), bf16)

    def pair_block(br, q2, k2, v2, sel):
        v2a = jnp.concatenate([v2, ones], axis=1)
        zero = jnp.zeros_like(q2)
        pv, l, m = [], [], []
        for e in range(2):
            qm = jnp.where(lo, q2, zero) if e == 0 else jnp.where(lo, zero, q2)
            s = lax.dot_general(qm, k2, _NT, preferred_element_type=f32)
            s = s + bias_ref[sel, br, e]
            me = jnp.max(s, axis=-1, keepdims=True)
            p = jnp.exp2(s - me)
            pva = jnp.dot(p.astype(bf16), v2a, preferred_element_type=f32)
            pv.append(pva[:, :LANES])
            l.append(pva[:, LANES:])
            m.append(jnp.broadcast_to(me, (BLOCK, LANES)))
        return (jnp.where(lo, pv[0], pv[1]), jnp.where(lo, l[0], l[1]),
                jnp.where(lo, m[0], m[1]))

    def window(cur, halo, j):
        if j == 0:
            return jnp.concatenate([halo, cur[0:BLOCK, :]], axis=0)
        return cur[(j - 1) * BLOCK:(j + 1) * BLOCK, :]

    def keep(slot, rows, res):
        pv_st[slot, rows, :], l_st[slot, rows, :], m_st[slot, rows, :] = res

    for r in range(16):
        keep(1, pl.ds(r, BLOCK, stride=16),
             pair_block(2, q16[0, r], window(k16.at[0, r], k16h[0, r], 0),
                        window(v16.at[0, r], v16h[0, r], 0), first))
    for r in range(4):
        for j in range(B_NSUB // 4):
            keep(0, pl.ds(j * 4 * BLOCK + r, BLOCK, stride=4),
                 pair_block(1, q4[0, r, j * BLOCK:(j + 1) * BLOCK, :],
                            window(k4.at[0, r], k4h[0, r], j), window(v4.at[0, r], v4h[0, r], j),
                            first if j == 0 else 0))
    for j in range(B_NSUB):
        rows = slice(j * BLOCK, (j + 1) * BLOCK)
        pvs, ls, ms = pair_block(0, q1[0, rows, :], window(k1.at[0], k1h[0], j),
                                 window(v1.at[0], v1h[0], j), first if j == 0 else 0)
        pvs, ls, ms = [pvs], [ls], [ms]
        for slot in range(2):
            pvs.append(pv_st[slot, rows, :])
            ls.append(l_st[slot, rows, :])
            ms.append(m_st[slot, rows, :])
        mx = jnp.maximum(jnp.maximum(ms[0], ms[1]), ms[2])
        num = den = None
        for n in range(3):
            w = jnp.exp2(ms[n] - mx)
            num = w * pvs[n] if num is None else num + w * pvs[n]
            den = w * ls[n] if den is None else den + w * ls[n]
        o_ref[0, rows, :] = num / den


def _attention_b(qkv, bias, batch):
    b1, b4, b16 = qkv
    S = b1.shape[0] // batch
    assert S % B_TILE == 0
    b1 = b1.reshape(batch, S, 3 * Q_W)
    npair = N_HEADS // 2
    n4, n16 = B_TILE // 4, B_TILE // 16
    assert n16 == BLOCK

    def specs(part):
        col = lambda hp: part * npair + hp
        nat = pl.BlockSpec((1, B_TILE, LANES), lambda bi, hp, ti: (bi, ti, col(hp)))
        nat_h = pl.BlockSpec((1, BLOCK, LANES), lambda bi, hp, ti: (
            bi, jnp.maximum(ti * B_NSUB - 1, 0), col(hp)))
        d4 = pl.BlockSpec((1, 4, n4, LANES), lambda bi, hp, ti: (bi, 0, ti, col(hp)))
        d4_h = pl.BlockSpec((1, 4, BLOCK, LANES), lambda bi, hp, ti: (
            bi, 0, jnp.maximum(ti * (n4 // BLOCK) - 1, 0), col(hp)))
        d16 = pl.BlockSpec((1, 16, n16, LANES), lambda bi, hp, ti: (bi, 0, ti, col(hp)))
        d16_h = pl.BlockSpec((1, 16, BLOCK, LANES), lambda bi, hp, ti: (
            bi, 0, jnp.maximum(ti - 1, 0), col(hp)))
        return nat, nat_h, d4, d4_h, d16, d16_h

    q_nat, _, q_d4, _, q_d16, _ = specs(0)
    kv_args = (b1, b1, b4, b4, b16, b16)
    return pl.pallas_call(
        _attn_b_kernel,
        grid=(batch, npair, S // B_TILE),
        in_specs=[q_nat, q_d4, q_d16, *specs(1), *specs(2),
                  pl.BlockSpec((2, 3, 2, BLOCK, 2 * BLOCK), lambda bi, hp, ti: (0, 0, hp, 0, 0))],
        out_specs=pl.BlockSpec((1, B_TILE, LANES), lambda bi, hp, ti: (bi, ti, hp)),
        out_shape=jax.ShapeDtypeStruct((batch, S, Q_W), f32),
        scratch_shapes=[pltpu.VMEM((2, B_TILE, LANES), f32),
                        pltpu.VMEM((2, B_TILE, LANES), f32),
                        pltpu.VMEM((2, B_TILE, LANES), f32)],
        compiler_params=pltpu.CompilerParams(
            dimension_semantics=("parallel", "parallel", "arbitrary"),
            vmem_limit_bytes=VMEM_LIMIT),
        name="attn_b",
    )(b1, b4, b16, *kv_args, *kv_args, bias)


def _outproj_kernel(x_ref, ya_ref, ob_ref, g_ref, w_ref, o_ref):
    ob = ob_ref[...]
    ms = jnp.mean(ob * ob, axis=-1, keepdims=True)
    yb = ((ob * lax.rsqrt(ms + EPS)) * g_ref[...]).astype(bf16)
    acc = jnp.dot(ya_ref[...], w_ref[0:Q_W, :], preferred_element_type=f32)
    acc = acc + jnp.dot(yb, w_ref[Q_W:, :], preferred_element_type=f32)
    o_ref[...] = x_ref[...] + acc


def _outproj(x2d, ya, ob, g_b, w_bf, *, tm=512):
    T, D = x2d.shape
    return pl.pallas_call(
        _outproj_kernel,
        grid=(T // tm,),
        in_specs=[
            pl.BlockSpec((tm, D), lambda i: (i, 0)),
            pl.BlockSpec((tm, Q_W), lambda i: (i, 0)),
            pl.BlockSpec((tm, Q_W), lambda i: (i, 0)),
            pl.BlockSpec((1, Q_W), lambda i: (0, 0)),
            pl.BlockSpec(w_bf.shape, lambda i: (0, 0), pipeline_mode=pl.Buffered(1)),
        ],
        out_specs=pl.BlockSpec((tm, D), lambda i: (i, 0)),
        out_shape=jax.ShapeDtypeStruct((T, D), f32),
        compiler_params=pltpu.CompilerParams(
            dimension_semantics=("parallel",), vmem_limit_bytes=VMEM_LIMIT),
        name="outproj",
    )(x2d, ya, ob, g_b, w_bf)


def _mlp_kernel(x_ref, g_ref, w1_ref, w2_ref, gf_ref, o_ref, h_scr, *, final_norm):
    f = pl.program_id(1)

    @pl.when(f == 0)
    def _():
        x = x_ref[...]
        ms = jnp.mean(x * x, axis=-1, keepdims=True)
        h_scr[...] = ((x * lax.rsqrt(ms + EPS)) * g_ref[...]).astype(bf16)
        o_ref[...] = x

    u = jnp.dot(h_scr[...], w1_ref[...], preferred_element_type=f32)
    u = jnp.maximum(u, 0.0)
    o_ref[...] += jnp.dot((u * u).astype(bf16), w2_ref[...], preferred_element_type=f32)

    if final_norm:
        @pl.when(f == pl.num_programs(1) - 1)
        def _():
            y = o_ref[...]
            ms = jnp.mean(y * y, axis=-1, keepdims=True)
            o_ref[...] = (y * lax.rsqrt(ms + EPS)) * gf_ref[...]


def _mlp(x2d, g, w1_bf, w2_bf, gf, *, final_norm, tm=1024, tf=1024):
    T, D = x2d.shape
    d_ff = w1_bf.shape[1]
    tm = min(tm, T)
    assert T % tm == 0 and d_ff % tf == 0
    return pl.pallas_call(
        functools.partial(_mlp_kernel, final_norm=final_norm),
        grid=(T // tm, d_ff // tf),
        in_specs=[
            pl.BlockSpec((tm, D), lambda i, f: (i, 0)),
            pl.BlockSpec((1, D), lambda i, f: (0, 0)),
            pl.BlockSpec((D, tf), lambda i, f: (0, f)),
            pl.BlockSpec((tf, D), lambda i, f: (f, 0)),
            pl.BlockSpec((1, D), lambda i, f: (0, 0)),
        ],
        out_specs=pl.BlockSpec((tm, D), lambda i, f: (i, 0)),
        out_shape=jax.ShapeDtypeStruct((T, D), f32),
        scratch_shapes=[pltpu.VMEM((tm, D), bf16)],
        compiler_params=pltpu.CompilerParams(
            dimension_semantics=("parallel", "arbitrary"), vmem_limit_bytes=VMEM_LIMIT),
        name="mlp",
    )(x2d, g, w1_bf, w2_bf, gf)


def kernel(x, g_attn, w_in, b_in, sinks_a, g_out_a, g_out_b, w_out, g_mlp, w_1, w_2, g_final):
    b, S, D = x.shape
    depth = w_in.shape[0]
    T = b * S
    bias_a = jnp.asarray(_band_bias(WINDOW_A - 1, 1))
    bias_b = jnp.asarray(np.stack([_band_bias(w // d, d) for w, d in DILATED_BRANCHES], axis=1))

    x2d = x.reshape(T, D)
    for l in range(depth):
        qa, kva, *qkv_b = _inproj(x2d, g_attn[l][None], w_in[l].astype(bf16), b_in[l][None], b)
        r3 = lambda t: t.reshape(b, S, t.shape[-1])
        ya = _attention_a(r3(qa), r3(kva), bias_a, sinks_a[l], g_out_a[l][None])
        ob = _attention_b(qkv_b, bias_b, b)
        x1 = _outproj(x2d, ya.reshape(T, Q_W), ob.reshape(T, Q_W), g_out_b[l][None],
                      w_out[l].astype(bf16))
        x2d = _mlp(x1, g_mlp[l][None], w_1[l].astype(bf16), w_2[l].astype(bf16), g_final[None],
                   final_norm=(l == depth - 1))
    return x2d.reshape(b, S, D)
```

```python
import functools

import numpy as np
import jax
import jax.numpy as jnp
from jax import lax
from jax.experimental import pallas as pl
from jax.experimental.pallas import tpu as pltpu

HEAD_DIM = 64
N_HEADS = 16
N_KV_A = 2
BLOCK = 128
WINDOW_A = 128
DILATED_BRANCHES = ((128, 1), (512, 4), (2048, 16))
EPS = 1e-5
NEG_INF = -1e30
LOG2E = 1.4426950408889634
Q_W = N_HEADS * HEAD_DIM
KV_A_W = N_KV_A * HEAD_DIM
LANES = 128
VMEM_LIMIT = 60 * 1024 * 1024

bf16 = jnp.bfloat16
f32 = jnp.float32
_NT = (((1,), (1,)), ((), ()))


def _alibi_slopes(n):
    return 2.0 ** (-8.0 * (np.arange(n) + 1) / n)


def _band_bias(max_steps, step_dist):
    qi = np.arange(BLOCK)[:, None]
    kj = np.arange(2 * BLOCK)[None, :]
    steps = qi + BLOCK - kj
    valid = (steps >= 0) & (steps <= max_steps)
    slopes = _alibi_slopes(N_HEADS).astype(np.float32)
    pen = slopes[:, None, None] * (step_dist * steps).astype(np.float32)[None]
    pen = pen.astype(np.float64) * LOG2E
    gen =np.where(valid[None], -pen, np.float32(NEG_INF)).astype(np.float32)
    first = np.where((valid & (kj >= BLOCK))[None], -pen, np.float32(NEG_INF)).astype(np.float32)
    return np.stack([gen, first])


def _inproj_kernel(x_ref, g_ref, w_ref, b_ref, qa_ref, kva_ref, b1_ref, b4_ref, b16_ref,
                   ys, ys4, *, chunk, row_chunk):
    tm = x_ref.shape[0]
    qscale = HEAD_DIM ** -0.5 * LOG2E
    n4, n16 = row_chunk // 4, row_chunk // 16
    c0 = Q_W + 2 * KV_A_W
    slot = 0
    for part in range(tm // row_chunk):
        rows = slice(part * row_chunk, (part + 1) * row_chunk)
        x = x_ref[rows, :]
        ms = jnp.mean(x * x, axis=-1, keepdims=True)
        h = ((x * lax.rsqrt(ms + EPS)) * g_ref[...]).astype(bf16)

        def project(col, ch, scale):
            y = jnp.dot(h, w_ref[:, col:col + ch], preferred_element_type=f32)
            y = y + b_ref[:, col:col + ch]
            return y if scale is None else y * scale

        for cc in range(0, Q_W, chunk):
            qa_ref[rows, cc:cc + chunk] = project(cc, chunk, qscale).astype(bf16)
        kva_ref[rows, :] = project(Q_W, 2 * KV_A_W, None).astype(bf16)

        rows4 = slice(part * n4, (part + 1) * n4)
        rows16 = slice(part * n16, (part + 1) * n16)
        for cc in range(0, 3 * Q_W, chunk):
            y = project(c0 + cc, chunk, qscale if cc < Q_W else None)
            b1_ref[rows, cc:cc + chunk] = y.astype(bf16)
            for s in range(chunk // LANES):
                lanes = slice(cc + s * LANES, cc + (s + 1) * LANES)
                ys[slot, s] = y[:, s * LANES:(s + 1) * LANES]
                for r in range(4):
                    t = ys[slot, s, pl.ds(r, n4, stride=4), :]
                    b4_ref[0, r, rows4, lanes] = t.astype(bf16)
                    ys4[slot, s, r * n4:(r + 1) * n4, :] = t
                for r in range(4):
                    for c in range(4):
                        b16_ref[0, r + 4 * c, rows16, lanes] = ys4[
                            slot, s, pl.ds(r * n4 + c, n16, stride=4), :].astype(bf16)
            slot = 1 - slot


def _inproj(x2d, g, w_bf, b_in, batch, *, tm=512, chunk=512, row_chunk=256):
    T, D = x2d.shape
    S = T // batch
    d_in = w_bf.shape[1]
    assert 2 * KV_A_W + 4 * Q_W == d_in and S % tm == 0 and Q_W % chunk == 0
    tiles = S // tm
    const = lambda i: (0, 0)
    row = lambda i: (i, 0)
    res = lambda i: (i // tiles, 0, i % tiles, 0)
    return pl.pallas_call(
        functools.partial(_inproj_kernel, chunk=chunk, row_chunk=row_chunk),
        grid=(T // tm,),
        in_specs=[
            pl.BlockSpec((tm, D), row),
            pl.BlockSpec((1, D), const),
            pl.BlockSpec((D, d_in), const, pipeline_mode=pl.Buffered(1)),
            pl.BlockSpec((1, d_in), const),
        ],
        out_specs=[pl.BlockSpec((tm, Q_W), row),
                   pl.BlockSpec((tm, 2 * KV_A_W), row),
                   pl.BlockSpec((tm, 3 * Q_W), row),
                   pl.BlockSpec((1, 4, tm // 4, 3 * Q_W), res),
                   pl.BlockSpec((1, 16, tm // 16, 3 * Q_W), res)],
        out_shape=[jax.ShapeDtypeStruct((T, Q_W), bf16),
                   jax.ShapeDtypeStruct((T, 2 * KV_A_W), bf16),
                   jax.ShapeDtypeStruct((T, 3 * Q_W), bf16),
                   jax.ShapeDtypeStruct((batch, 4, S // 4, 3 * Q_W), bf16),
                   jax.ShapeDtypeStruct((batch, 16, S // 16, 3 * Q_W), bf16)],
        scratch_shapes=[pltpu.VMEM((2, chunk // LANES, row_chunk, LANES), f32),
                        pltpu.VMEM((2, chunk // LANES, row_chunk, LANES), f32)],
        compiler_params=pltpu.CompilerParams(
            dimension_semantics=("parallel",), vmem_limit_bytes=VMEM_LIMIT),
        name="inproj",
    )(x2d, g, w_bf, b_in)


def _attn_a_kernel(q_ref, k_ref, kp_ref, v_ref, vp_ref, bias_ref, sink_ref, g_ref, out_ref,
                   k_scr, v_scr, o_scr, *, tq):
    li = pl.program_id(1)
    lane = lax.broadcasted_iota(jnp.int32, (1, LANES), 1)
    lo = lane < HEAD_DIM

    def dup(t):
        r = pltpu.roll(t.astype(f32), HEAD_DIM, 1).astype(bf16)
        return jnp.concatenate([jnp.where(lo, t, r), jnp.where(lo, r, t)], axis=1)
    k_scr[0:BLOCK, :] = dup(kp_ref[0])
    k_scr[BLOCK:, :] = dup(k_ref[0])
    v_scr[0:BLOCK, :] = dup(vp_ref[0])
    v_scr[BLOCK:, :] = dup(v_ref[0])

    first = jnp.where(li == 0, 1, 0)
    ones = jnp.ones((2 * BLOCK, LANES), bf16)
    for j in range(tq // BLOCK):
        rows = slice(j * BLOCK, (j + 1) * BLOCK)
        krows = slice(j * BLOCK, (j + 2) * BLOCK)
        sel = first if j == 0 else 0
        for hp in range(N_HEADS // 2):
            cols = slice(hp * LANES, (hp + 1) * LANES)
            g = (2 * hp) // (N_HEADS // N_KV_A)
            kcols = slice(g * LANES, (g + 1) * LANES)
            q2 = q_ref[0, rows, cols]
            k2 = k_scr[krows, kcols]
            v2a = jnp.concatenate([v_scr[krows, kcols], ones], axis=1)
            zero = jnp.zeros_like(q2)
            pv, l = [], []
            for e in range(2):
                h = 2 * hp + e
                qm = jnp.where(lo, q2, zero) if e == 0 else jnp.where(lo, zero, q2)
                s = lax.dot_general(qm, k2, _NT, preferred_element_type=f32)
                s = s + bias_ref[sel, h]
                sink = sink_ref[h] * LOG2E
                m = jnp.maximum(jnp.max(s, axis=-1, keepdims=True), sink)
                p = jnp.exp2(s - m)
                pva = jnp.dot(p.astype(bf16), v2a, preferred_element_type=f32)
                pv.append(pva[:, :LANES])
                l.append(pva[:, LANES:] + jnp.exp2(sink - m))
            o_scr[rows, cols] = jnp.where(lo, pv[0], pv[1]) / jnp.where(lo, l[0], l[1])

    o = o_scr[...]
    ms = jnp.mean(o * o, axis=-1, keepdims=True)
    out_ref[0] = ((o * lax.rsqrt(ms + EPS)) * g_ref[...]).astype(bf16)


def _attention_a(q, kv, bias, sinks, gain, *, tq=512):
    b, S, _ = q.shape
    tq = min(tq, S)
    assert S % tq == 0 and tq % BLOCK == 0
    nsub = tq // BLOCK
    cur = lambda bi, li: (bi, li, 0)
    k, v = kv, kv
    return pl.pallas_call(
        functools.partial(_attn_a_kernel, tq=tq),
        grid=(b, S // tq),
        in_specs=[
            pl.BlockSpec((1, tq, Q_W), cur),
            pl.BlockSpec((1, tq, KV_A_W), lambda bi, li: (bi, li, 0)),
            pl.BlockSpec((1, BLOCK, KV_A_W), lambda bi, li: (bi, jnp.maximum(li * nsub - 1, 0), 0)),
            pl.BlockSpec((1, tq, KV_A_W), lambda bi, li: (bi, li, 1)),
            pl.BlockSpec((1, BLOCK, KV_A_W), lambda bi, li: (bi, jnp.maximum(li * nsub - 1, 0), 1)),
            pl.BlockSpec(bias.shape, lambda bi, li: (0, 0, 0, 0)),
            pl.BlockSpec(memory_space=pltpu.SMEM),
            pl.BlockSpec((1, Q_W), lambda bi, li: (0, 0)),
        ],
        out_specs=pl.BlockSpec((1, tq, Q_W), cur),
        out_shape=jax.ShapeDtypeStruct((b, S, Q_W), bf16),
        scratch_shapes=[pltpu.VMEM((tq + BLOCK, 2 * KV_A_W), bf16),
                        pltpu.VMEM((tq + BLOCK, 2 * KV_A_W), bf16),
                        pltpu.VMEM((tq, Q_W), f32)],
        compiler_params=pltpu.CompilerParams(
            dimension_semantics=("parallel", "arbitrary"), vmem_limit_bytes=VMEM_LIMIT),
        name="attn_a",
    )(q, k, k, v, v, bias, sinks, gain)


B_TILE = 2048
B_DILS = tuple(d for _, d in DILATED_BRANCHES)
assert B_DILS == (1, 4, 16) and all(w // d == BLOCK for w, d in DILATED_BRANCHES)
B_NSUB = B_TILE // BLOCK


def _attn_b_kernel(q1, q4, q16, k1, k1h, k4, k4h, k16, k16h, v1, v1h, v4, v4h, v16, v16h,
                   bias_ref, o_ref, pv_st, l_st, m_st):
    first = jnp.where(pl.program_id(2) == 0, 1, 0)
    lane = lax.broadcasted_iota(jnp.int32, (1, LANES), 1)
    lo = lane < HEAD_DIM
    ones = jnp.ones((2 * BLOCK, LANES), bf16)

    def pair_block(br, q2, k2, v2, sel):
        v2a = jnp.concatenate([v2, ones], axis=1)
        zero = jnp.zeros_like(q2)
        pv, l, m = [], [], []
        for e in range(2):
            qm = jnp.where(lo, q2, zero) if e == 0 else jnp.where(lo, zero, q2)
            s = lax.dot_general(qm, k2, _NT, preferred_element_type=f32)
            s = s + bias_ref[sel, br, e]
            me = jnp.max(s, axis=-1, keepdims=True)
            p = jnp.exp2(s - me)
            pva = jnp.dot(p.astype(bf16), v2a, preferred_element_type=f32)
            pv.append(pva[:, :LANES])
            l.append(pva[:, LANES:])
            m.append(jnp.broadcast_to(me, (BLOCK, LANES)))
        return (jnp.where(lo, pv[0], pv[1]), jnp.where(lo, l[0], l[1]),
                jnp.where(lo, m[0], m[1]))

    def window(cur, halo, j):
        if j == 0:
            return jnp.concatenate([halo, cur[0:BLOCK, :]], axis=0)
        return cur[(j - 1) * BLOCK:(j + 1) * BLOCK, :]

    def keep(slot, rows, res):
        pv_st[slot, rows, :], l_st[slot, rows, :], m_st[slot, rows, :] = res

    for r in range(16):
        keep(1, pl.ds(r, BLOCK, stride=16),
             pair_block(2, q16[0, r], window(k16.at[0, r], k16h[0, r], 0),
                        window(v16.at[0, r], v16h[0, r], 0), first))
    for r in range(4):
        for j in range(B_NSUB // 4):
            keep(0, pl.ds(j * 4 * BLOCK + r, BLOCK, stride=4),
                 pair_block(1, q4[0, r, j * BLOCK:(j + 1) * BLOCK, :],
                            window(k4.at[0, r], k4h[0, r], j), window(v4.at[0, r], v4h[0, r], j),
                            first if j == 0 else 0))
    for j in range(B_NSUB):
        rows = slice(j * BLOCK, (j + 1) * BLOCK)
        pvs, ls, ms = pair_block(0, q1[0, rows, :], window(k1.at[0], k1h[0], j),
                                 window(v1.at[0], v1h[0], j), first if j == 0 else 0)
        pvs, ls, ms = [pvs], [ls], [ms]
        for slot in range(2):
            pvs.append(pv_st[slot, rows, :])
            ls.append(l_st[slot, rows, :])
            ms.append(m_st[slot, rows, :])
        mx = jnp.maximum(jnp.maximum(ms[0], ms[1]), ms[2])
        num = den = None
        for n in range(3):
            w = jnp.exp2(ms[n] - mx)
            num = w * pvs[n] if num is None else num + w * pvs[n]
            den = w * ls[n] if den is None else den + w * ls[n]
        o_ref[0, rows, :] = num / den


def _attention_b(qkv, bias, batch):
    b1, b4, b16 = qkv
    S = b1.shape[0] // batch
    assert S % B_TILE == 0
    b1 = b1.reshape(batch, S, 3 * Q_W)
    npair = N_HEADS // 2
    n4, n16 = B_TILE // 4, B_TILE // 16
    assert n16 == BLOCK

    def specs(part):
        col = lambda hp: part * npair + hp
        nat = pl.BlockSpec((1, B_TILE, LANES), lambda bi, hp, ti: (bi, ti, col(hp)))
        nat_h = pl.BlockSpec((1, BLOCK, LANES), lambda bi, hp, ti: (
            bi, jnp.maximum(ti * B_NSUB - 1, 0), col(hp)))
        d4 = pl.BlockSpec((1, 4, n4, LANES), lambda bi, hp, ti: (bi, 0, ti, col(hp)))
        d4_h = pl.BlockSpec((1, 4, BLOCK, LANES), lambda bi, hp, ti: (
            bi, 0, jnp.maximum(ti * (n4 // BLOCK) - 1, 0), col(hp)))
        d16 = pl.BlockSpec((1, 16, n16, LANES), lambda bi, hp, ti: (bi, 0, ti, col(hp)))
        d16_h = pl.BlockSpec((1, 16, BLOCK, LANES), lambda bi, hp, ti: (
            bi, 0, jnp.maximum(ti - 1, 0), col(hp)))
        return nat, nat_h, d4, d4_h, d16, d16_h

    q_nat, _, q_d4, _, q_d16, _ = specs(0)
    kv_args = (b1, b1, b4, b4, b16, b16)
    return pl.pallas_call(
        _attn_b_kernel,
        grid=(batch, npair, S // B_TILE),
        in_specs=[q_nat, q_d4, q_d16, *specs(1), *specs(2),
                  pl.BlockSpec((2, 3, 2, BLOCK, 2 * BLOCK), lambda bi, hp, ti: (0, 0, hp, 0, 0))],
        out_specs=pl.BlockSpec((1, B_TILE, LANES), lambda bi, hp, ti: (bi, ti, hp)),
        out_shape=jax.ShapeDtypeStruct((batch, S, Q_W), f32),
        scratch_shapes=[pltpu.VMEM((2, B_TILE, LANES), f32),
                        pltpu.VMEM((2, B_TILE, LANES), f32),
                        pltpu.VMEM((2, B_TILE, LANES), f32)],
        compiler_params=pltpu.CompilerParams(
            dimension_semantics=("parallel", "parallel", "arbitrary"),
            vmem_limit_bytes=VMEM_LIMIT),
        name="attn_b",
    )(b1, b4, b16, *kv_args, *kv_args, bias)


def _outproj_kernel(x_ref, ya_ref, ob_ref, g_ref, w_ref, o_ref):
    ob = ob_ref[...]
    ms = jnp.mean(ob * ob, axis=-1, keepdims=True)
    yb = ((ob * lax.rsqrt(ms + EPS)) * g_ref[...]).astype(bf16)
    acc = jnp.dot(ya_ref[...], w_ref[0:Q_W, :], preferred_element_type=f32)
    acc = acc + jnp.dot(yb, w_ref[Q_W:, :], preferred_element_type=f32)
    o_ref[...] = x_ref[...] + acc


def _outproj(x2d, ya, ob, g_b, w_bf, *, tm=512):
    T, D = x2d.shape
    return pl.pallas_call(
        _outproj_kernel,
        grid=(T // tm,),
        in_specs=[
            pl.BlockSpec((tm, D), lambda i: (i, 0)),
            pl.BlockSpec((tm, Q_W), lambda i: (i, 0)),
            pl.BlockSpec((tm, Q_W), lambda i: (i, 0)),
            pl.BlockSpec((1, Q_W), lambda i: (0, 0)),
            pl.BlockSpec(w_bf.shape, lambda i: (0, 0), pipeline_mode=pl.Buffered(1)),
        ],
        out_specs=pl.BlockSpec((tm, D), lambda i: (i, 0)),
        out_shape=jax.ShapeDtypeStruct((T, D), f32),
        compiler_params=pltpu.CompilerParams(
            dimension_semantics=("parallel",), vmem_limit_bytes=VMEM_LIMIT),
        name="outproj",
    )(x2d, ya, ob, g_b, w_bf)


def _mlp_kernel(x_ref, g_ref, w1_ref, w2_ref, gf_ref, o_ref, h_scr, *, final_norm, row_chunk):
    f = pl.program_id(1)
    last = pl.num_programs(1) - 1
    tm = o_ref.shape[0]
    chunks = [slice(r, r + row_chunk) for r in range(0, tm, row_chunk)]

    def ff(h):
        u = jnp.maximum(jnp.dot(h, w1_ref[...], preferred_element_type=f32), 0.0)
        return jnp.dot((u * u).astype(bf16), w2_ref[...], preferred_element_type=f32)

    @pl.when(f == 0)
    def _():
        for rows in chunks:
            x = x_ref[rows, :]
            ms = jnp.mean(x * x, axis=-1, keepdims=True)
            h = ((x * lax.rsqrt(ms + EPS)) * g_ref[...]).astype(bf16)
            h_scr[rows, :] = h
            o_ref[rows, :] = x + ff(h)

    @pl.when(jnp.logical_and(f > 0, jnp.logical_or(f < last, not final_norm)))
    def _():
        o_ref[...] += ff(h_scr[...])

    if final_norm:
        @pl.when(jnp.logical_and(f > 0, f == last))
        def _():
            for rows in chunks:
                y = o_ref[rows, :] + ff(h_scr[rows, :])
                ms = jnp.mean(y * y, axis=-1, keepdims=True)
                o_ref[rows, :] = (y * lax.rsqrt(ms + EPS)) * gf_ref[...]


def _mlp(x2d, g, w1_bf, w2_bf, gf, *, final_norm, tm=1024, tf=1024, row_chunk=256):
    T, D = x2d.shape
    d_ff = w1_bf.shape[1]
    tm = min(tm, T)
    assert T % tm == 0 and d_ff % tf == 0 and d_ff // tf >= 2 and tm % row_chunk == 0
    return pl.pallas_call(
        functools.partial(_mlp_kernel, final_norm=final_norm, row_chunk=row_chunk),
        grid=(T // tm, d_ff // tf),
        in_specs=[
            pl.BlockSpec((tm, D), lambda i, f: (i, 0)),
            pl.BlockSpec((1, D), lambda i, f: (0, 0)),
            pl.BlockSpec((D, tf), lambda i, f: (0, f)),
            pl.BlockSpec((tf, D), lambda i, f: (f, 0)),
            pl.BlockSpec((1, D), lambda i, f: (0, 0)),
        ],
        out_specs=pl.BlockSpec((tm, D), lambda i, f: (i, 0)),
        out_shape=jax.ShapeDtypeStruct((T, D), f32),
        scratch_shapes=[pltpu.VMEM((tm, D), bf16)],
        compiler_params=pltpu.CompilerParams(
            dimension_semantics=("parallel", "arbitrary"), vmem_limit_bytes=VMEM_LIMIT),
        name="mlp",
    )(x2d, g, w1_bf, w2_bf, gf)


def kernel(x, g_attn, w_in, b_in, sinks_a, g_out_a, g_out_b, w_out, g_mlp, w_1, w_2, g_final):
    b, S, D = x.shape
    depth = w_in.shape[0]
    T = b * S
    bias_a = jnp.asarray(_band_bias(WINDOW_A - 1, 1))
    bias_b = jnp.asarray(np.stack([_band_bias(w // d, d) for w, d in DILATED_BRANCHES], axis=1))

    x2d = x.reshape(T, D)
    for l in range(depth):
        qa, kva, *qkv_b = _inproj(x2d, g_attn[l][None], w_in[l].astype(bf16), b_in[l][None], b)
        r3 = lambda t: t.reshape(b, S, t.shape[-1])
        ya = _attention_a(r3(qa), r3(kva), bias_a, sinks_a[l], g_out_a[l][None])
        ob = _attention_b(qkv_b, bias_b, b)
        x1 = _outproj(x2d, ya.reshape(T, Q_W), ob.reshape(T, Q_W), g_out_b[l][None],
                      w_out[l].astype(bf16))
        x2d = _mlp(x1, g_mlp[l][None], w_1[l].astype(bf16), w_2[l].astype(bf16), g_final[None],
                   final_norm=(l == depth - 1))
    return x2d.reshape(b, S, D)
```

```python
import functools

import numpy as np
import jax
import jax.numpy as jnp
from jax import lax
from jax.experimental import pallas as pl
from jax.experimental.pallas import tpu as pltpu

HEAD_DIM = 64
N_HEADS = 16
N_KV_A = 2
BLOCK = 128
WINDOW_A = 128
DILATED_BRANCHES = ((128, 1), (512, 4), (2048, 16))
EPS = 1e-5
NEG_INF = -1e30
LOG2E = 1.4426950408889634
Q_W = N_HEADS * HEAD_DIM
KV_A_W = N_KV_A * HEAD_DIM
LANES = 128
VMEM_LIMIT = 60 * 1024 * 1024

bf16 = jnp.bfloat16
f32 = jnp.float32
_NT = (((1,), (1,)), ((), ()))


def _alibi_slopes(n):
    return 2.0 ** (-8.0 * (np.arange(n) + 1) / n)


def _band_bias(max_steps, step_dist):
    qi = np.arange(BLOCK)[:, None]
    kj = np.arange(2 * BLOCK)[None, :]
    steps = qi + BLOCK - kj
    valid = (steps >= 0) & (steps <= max_steps)
    slopes = _alibi_slopes(N_HEADS).astype(np.float32)
    pen = slopes[:, None, None] * (step_dist * steps).astype(np.float32)[None]
    pen = pen.astype(np.float64) * LOG2E
    gen =np.where(valid[None], -pen, np.float32(NEG_INF)).astype(np.float32)
    first = np.where((valid & (kj >= BLOCK))[None], -pen, np.float32(NEG_INF)).astype(np.float32)
    return np.stack([gen, first])


def _inproj_kernel(x_ref, g_ref, w_ref, b_ref, qa_ref, kva_ref, b1_ref, b4_ref, b16_ref,
                   ys, ys4, *, chunk, row_chunk):
    tm = x_ref.shape[0]
    qscale = HEAD_DIM ** -0.5 * LOG2E
    n4, n16 = row_chunk // 4, row_chunk // 16
    c0 = Q_W + 2 * KV_A_W
    slot = 0
    for part in range(tm // row_chunk):
        rows = slice(part * row_chunk, (part + 1) * row_chunk)
        x = x_ref[rows, :]
        ms = jnp.mean(x * x, axis=-1, keepdims=True)
        h = ((x * lax.rsqrt(ms + EPS)) * g_ref[...]).astype(bf16)

        def project(col, ch, scale):
            y = jnp.dot(h, w_ref[:, col:col + ch], preferred_element_type=f32)
            y = y + b_ref[:, col:col + ch]
            return y if scale is None else y * scale

        for cc in range(0, Q_W, chunk):
            qa_ref[rows, cc:cc + chunk] = project(cc, chunk, qscale).astype(bf16)
        kva_ref[rows, :] = project(Q_W, 2 * KV_A_W, None).astype(bf16)

        rows4 = slice(part * n4, (part + 1) * n4)
        rows16 = slice(part * n16, (part + 1) * n16)
        for cc in range(0, 3 * Q_W, chunk):
            y = project(c0 + cc, chunk, qscale if cc < Q_W else None)
            b1_ref[rows, cc:cc + chunk] = y.astype(bf16)
            for s in range(chunk // LANES):
                lanes = slice(cc + s * LANES, cc + (s + 1) * LANES)
                ys[slot, s] = y[:, s * LANES:(s + 1) * LANES]
                for r in range(4):
                    t = ys[slot, s, pl.ds(r, n4, stride=4), :]
                    b4_ref[0, r, rows4, lanes] = t.astype(bf16)
                    ys4[slot, s, r * n4:(r + 1) * n4, :] = t
                for r in range(4):
                    for c in range(4):
                        b16_ref[0, r + 4 * c, rows16, lanes] = ys4[
                            slot, s, pl.ds(r * n4 + c, n16, stride=4), :].astype(bf16)
            slot = 1 - slot


def _inproj(x2d, g, w_bf, b_in, batch, *, tm=512, chunk=512, row_chunk=512):
    T, D = x2d.shape
    S = T // batch
    d_in = w_bf.shape[1]
    assert 2 * KV_A_W + 4 * Q_W == d_in and S % tm == 0 and Q_W % chunk == 0
    tiles = S // tm
    const = lambda i: (0, 0)
    row = lambda i: (i, 0)
    res = lambda i: (i // tiles, 0, i % tiles, 0)
    return pl.pallas_call(
        functools.partial(_inproj_kernel, chunk=chunk, row_chunk=row_chunk),
        grid=(T // tm,),
        in_specs=[
            pl.BlockSpec((tm, D), row),
            pl.BlockSpec((1, D), const),
            pl.BlockSpec((D, d_in), const, pipeline_mode=pl.Buffered(1)),
            pl.BlockSpec((1, d_in), const),
        ],
        out_specs=[pl.BlockSpec((tm, Q_W), row),
                   pl.BlockSpec((tm, 2 * KV_A_W), row),
                   pl.BlockSpec((tm, 3 * Q_W), row),
                   pl.BlockSpec((1, 4, tm // 4, 3 * Q_W), res),
                   pl.BlockSpec((1, 16, tm // 16, 3 * Q_W), res)],
        out_shape=[jax.ShapeDtypeStruct((T, Q_W), bf16),
                   jax.ShapeDtypeStruct((T, 2 * KV_A_W), bf16),
                   jax.ShapeDtypeStruct((T, 3 * Q_W), bf16),
                   jax.ShapeDtypeStruct((batch, 4, S // 4, 3 * Q_W), bf16),
                   jax.ShapeDtypeStruct((batch, 16, S // 16, 3 * Q_W), bf16)],
        scratch_shapes=[pltpu.VMEM((2, chunk // LANES, row_chunk, LANES), f32),
                        pltpu.VMEM((2, chunk // LANES, row_chunk, LANES), f32)],
        compiler_params=pltpu.CompilerParams(
            dimension_semantics=("parallel",), vmem_limit_bytes=VMEM_LIMIT),
        name="inproj",
    )(x2d, g, w_bf, b_in)


def _attn_a_kernel(q_ref, k_ref, kp_ref, v_ref, vp_ref, bias_ref, sink_ref, g_ref, out_ref,
                   k_scr, v_scr, o_scr, *, tq):
    li = pl.program_id(1)
    lane = lax.broadcasted_iota(jnp.int32, (1, LANES), 1)
    lo = lane < HEAD_DIM

    def dup(t):
        r = pltpu.roll(t.astype(f32), HEAD_DIM, 1).astype(bf16)
        return jnp.concatenate([jnp.where(lo, t, r), jnp.where(lo, r, t)], axis=1)
    k_scr[0:BLOCK, :] = dup(kp_ref[0])
    k_scr[BLOCK:, :] = dup(k_ref[0])
    v_scr[0:BLOCK, :] = dup(vp_ref[0])
    v_scr[BLOCK:, :] = dup(v_ref[0])

    first = jnp.where(li == 0, 1, 0)
    ones = jnp.ones((2 * BLOCK, LANES), bf16)
    for j in range(tq // BLOCK):
        rows = slice(j * BLOCK, (j + 1) * BLOCK)
        krows = slice(j * BLOCK, (j + 2) * BLOCK)
        sel = first if j == 0 else 0
        for hp in range(N_HEADS // 2):
            cols = slice(hp * LANES, (hp + 1) * LANES)
            g = (2 * hp) // (N_HEADS // N_KV_A)
            kcols = slice(g * LANES, (g + 1) * LANES)
            q2 = q_ref[0, rows, cols]
            k2 = k_scr[krows, kcols]
            v2a = jnp.concatenate([v_scr[krows, kcols], ones], axis=1)
            zero = jnp.zeros_like(q2)
            pv, l = [], []
            for e in range(2):
                h = 2 * hp + e
                qm = jnp.where(lo, q2, zero) if e == 0 else jnp.where(lo, zero, q2)
                s = lax.dot_general(qm, k2, _NT, preferred_element_type=f32)
                s = s + bias_ref[sel, h]
                sink = sink_ref[h] * LOG2E
                m = jnp.maximum(jnp.max(s, axis=-1, keepdims=True), sink)
                p = jnp.exp2(s - m)
                pva = jnp.dot(p.astype(bf16), v2a, preferred_element_type=f32)
                pv.append(pva[:, :LANES])
                l.append(pva[:, LANES:] + jnp.exp2(sink - m))
            o_scr[rows, cols] = jnp.where(lo, pv[0], pv[1]) / jnp.where(lo, l[0], l[1])

    o = o_scr[...]
    ms = jnp.mean(o * o, axis=-1, keepdims=True)
    out_ref[0] = ((o * lax.rsqrt(ms + EPS)) * g_ref[...]).astype(bf16)


def _attention_a(q, kv, bias, sinks, gain, *, tq=512):
    b, S, _ = q.shape
    tq = min(tq, S)
    assert S % tq == 0 and tq % BLOCK == 0
    nsub = tq // BLOCK
    cur = lambda bi, li: (bi, li, 0)
    k, v = kv, kv
    return pl.pallas_call(
        functools.partial(_attn_a_kernel, tq=tq),
        grid=(b, S // tq),
        in_specs=[
            pl.BlockSpec((1, tq, Q_W), cur),
            pl.BlockSpec((1, tq, KV_A_W), lambda bi, li: (bi, li, 0)),
            pl.BlockSpec((1, BLOCK, KV_A_W), lambda bi, li: (bi, jnp.maximum(li * nsub - 1, 0), 0)),
            pl.BlockSpec((1, tq, KV_A_W), lambda bi, li: (bi, li, 1)),
            pl.BlockSpec((1, BLOCK, KV_A_W), lambda bi, li: (bi, jnp.maximum(li * nsub - 1, 0), 1)),
            pl.BlockSpec(bias.shape, lambda bi, li: (0, 0, 0, 0)),
            pl.BlockSpec(memory_space=pltpu.SMEM),
            pl.BlockSpec((1, Q_W), lambda bi, li: (0, 0)),
        ],
        out_specs=pl.BlockSpec((1, tq, Q_W), cur),
        out_shape=jax.ShapeDtypeStruct((b, S, Q_W), bf16),
        scratch_shapes=[pltpu.VMEM((tq + BLOCK, 2 * KV_A_W), bf16),
                        pltpu.VMEM((tq + BLOCK, 2 * KV_A_W), bf16),
                        pltpu.VMEM((tq, Q_W), f32)],
        compiler_params=pltpu.CompilerParams(
            dimension_semantics=("parallel", "arbitrary"), vmem_limit_bytes=VMEM_LIMIT),
        name="attn_a",
    )(q, k, k, v, v, bias, sinks, gain)


B_TILE = 2048
B_DILS = tuple(d for _, d in DILATED_BRANCHES)
assert B_DILS == (1, 4, 16) and all(w // d == BLOCK for w, d in DILATED_BRANCHES)
B_NSUB = B_TILE // BLOCK


def _attn_b_kernel(q1, q4, q16, k1, k1h, k4, k4h, k16, k16h, v1, v1h, v4, v4h, v16, v16h,
                   bias_ref, o_ref, pv_st, l_st, m_st):
    first = jnp.where(pl.program_id(2) == 0, 1, 0)
    lane = lax.broadcasted_iota(jnp.int32, (1, LANES), 1)
    lo = lane < HEAD_DIM
    ones = jnp.ones((2 * BLOCK, LANES), bf16)

    def pair_block(br, q2, k2, v2, sel):
        v2a = jnp.concatenate([v2, ones], axis=1)
        zero = jnp.zeros_like(q2)
        pv, l, m = [], [], []
        for e in range(2):
            qm = jnp.where(lo, q2, zero) if e == 0 else jnp.where(lo, zero, q2)
            s = lax.dot_general(qm, k2, _NT, preferred_element_type=f32)
            s = s + bias_ref[sel, br, e]
            me = jnp.max(s, axis=-1, keepdims=True)
            p = jnp.exp2(s - me)
            pva = jnp.dot(p.astype(bf16), v2a, preferred_element_type=f32)
            pv.append(pva[:, :LANES])
            l.append(pva[:, LANES:])
            m.append(jnp.broadcast_to(me, (BLOCK, LANES)))
        return (jnp.where(lo, pv[0], pv[1]), jnp.where(lo, l[0], l[1]),
                jnp.where(lo, m[0], m[1]))

    def window(cur, halo, j):
        if j == 0:
            return jnp.concatenate([halo, cur[0:BLOCK, :]], axis=0)
        return cur[(j - 1) * BLOCK:(j + 1) * BLOCK, :]

    def keep(slot, rows, res):
        pv_st[slot, rows, :], l_st[slot, rows, :], m_st[slot, rows, :] = res

    n4 = B_TILE // 4
    for r in range(4):
        for c in range(4):
            keep(2, pl.ds(r * n4 + c, BLOCK, stride=4),
                 pair_block(2, q16[0, r + 4 * c],
                            window(k16.at[0, r + 4 * c], k16h[0, r + 4 * c], 0),
                            window(v16.at[0, r + 4 * c], v16h[0, r + 4 * c], 0), first))
        for j in range(n4 // BLOCK):
            mid = slice(r * n4 + j * BLOCK, r * n4 + (j + 1) * BLOCK)
            keep(1, pl.ds(j * 4 * BLOCK + r, BLOCK, stride=4),
                 (pv_st[2, mid, :], l_st[2, mid, :], m_st[2, mid, :]))
    for r in range(4):
        for j in range(B_NSUB // 4):
            keep(0, pl.ds(j * 4 * BLOCK + r, BLOCK, stride=4),
                 pair_block(1, q4[0, r, j * BLOCK:(j + 1) * BLOCK, :],
                            window(k4.at[0, r], k4h[0, r], j), window(v4.at[0, r], v4h[0, r], j),
                            first if j == 0 else 0))
    for j in range(B_NSUB):
        rows = slice(j * BLOCK, (j + 1) * BLOCK)
        pvs, ls, ms = pair_block(0, q1[0, rows, :], window(k1.at[0], k1h[0], j),
                                 window(v1.at[0], v1h[0], j), first if j == 0 else 0)
        pvs, ls, ms = [pvs], [ls], [ms]
        for slot in range(2):
            pvs.append(pv_st[slot, rows, :])
            ls.append(l_st[slot, rows, :])
            ms.append(m_st[slot, rows, :])
        mx = jnp.maximum(jnp.maximum(ms[0], ms[1]), ms[2])
        num = den = None
        for n in range(3):
            w = jnp.exp2(ms[n] - mx)
            num = w * pvs[n] if num is None else num + w * pvs[n]
            den = w * ls[n] if den is None else den + w * ls[n]
        o_ref[0, rows, :] = num / den


def _attention_b(qkv, bias, batch):
    b1, b4, b16 = qkv
    S = b1.shape[0] // batch
    assert S % B_TILE == 0
    b1 = b1.reshape(batch, S, 3 * Q_W)
    npair = N_HEADS // 2
    n4, n16 = B_TILE // 4, B_TILE // 16
    assert n16 == BLOCK

    def specs(part):
        col = lambda hp: part * npair + hp
        nat = pl.BlockSpec((1, B_TILE, LANES), lambda bi, hp, ti: (bi, ti, col(hp)))
        nat_h = pl.BlockSpec((1, BLOCK, LANES), lambda bi, hp, ti: (
            bi, jnp.maximum(ti * B_NSUB - 1, 0), col(hp)))
        d4 = pl.BlockSpec((1, 4, n4, LANES), lambda bi, hp, ti: (bi, 0, ti, col(hp)))
        d4_h = pl.BlockSpec((1, 4, BLOCK, LANES), lambda bi, hp, ti: (
            bi, 0, jnp.maximum(ti * (n4 // BLOCK) - 1, 0), col(hp)))
        d16 = pl.BlockSpec((1, 16, n16, LANES), lambda bi, hp, ti: (bi, 0, ti, col(hp)))
        d16_h = pl.BlockSpec((1, 16, BLOCK, LANES), lambda bi, hp, ti: (
            bi, 0, jnp.maximum(ti - 1, 0), col(hp)))
        return nat, nat_h, d4, d4_h, d16, d16_h

    q_nat, _, q_d4, _, q_d16, _ = specs(0)
    kv_args = (b1, b1, b4, b4, b16, b16)
    return pl.pallas_call(
        _attn_b_kernel,
        grid=(batch, npair, S // B_TILE),
        in_specs=[q_nat, q_d4, q_d16, *specs(1), *specs(2),
                  pl.BlockSpec((2, 3, 2, BLOCK, 2 * BLOCK), lambda bi, hp, ti: (0, 0, hp, 0, 0))],
        out_specs=pl.BlockSpec((1, B_TILE, LANES), lambda bi, hp, ti: (bi, ti, hp)),
        out_shape=jax.ShapeDtypeStruct((batch, S, Q_W), f32),
        scratch_shapes=[pltpu.VMEM((3, B_TILE, LANES), f32),
                        pltpu.VMEM((3, B_TILE, LANES), f32),
                        pltpu.VMEM((3, B_TILE, LANES), f32)],
        compiler_params=pltpu.CompilerParams(
            dimension_semantics=("parallel", "parallel", "arbitrary"),
            vmem_limit_bytes=VMEM_LIMIT),
        name="attn_b",
    )(b1, b4, b16, *kv_args, *kv_args, bias)


def _outproj_kernel(x_ref, ya_ref, ob_ref, g_ref, w_ref, o_ref):
    ob = ob_ref[...]
    ms = jnp.mean(ob * ob, axis=-1, keepdims=True)
    yb = ((ob * lax.rsqrt(ms + EPS)) * g_ref[...]).astype(bf16)
    acc = jnp.dot(ya_ref[...], w_ref[0:Q_W, :], preferred_element_type=f32)
    acc = acc + jnp.dot(yb, w_ref[Q_W:, :], preferred_element_type=f32)
    o_ref[...] = x_ref[...] + acc


def _outproj(x2d, ya, ob, g_b, w_bf, *, tm=512):
    T, D = x2d.shape
    return pl.pallas_call(
        _outproj_kernel,
        grid=(T // tm,),
        in_specs=[
            pl.BlockSpec((tm, D), lambda i: (i, 0)),
            pl.BlockSpec((tm, Q_W), lambda i: (i, 0)),
            pl.BlockSpec((tm, Q_W), lambda i: (i, 0)),
            pl.BlockSpec((1, Q_W), lambda i: (0, 0)),
            pl.BlockSpec(w_bf.shape, lambda i: (0, 0), pipeline_mode=pl.Buffered(1)),
        ],
        out_specs=pl.BlockSpec((tm, D), lambda i: (i, 0)),
        out_shape=jax.ShapeDtypeStruct((T, D), f32),
        compiler_params=pltpu.CompilerParams(
            dimension_semantics=("parallel",), vmem_limit_bytes=VMEM_LIMIT),
        name="outproj",
    )(x2d, ya, ob, g_b, w_bf)


def _mlp_kernel(x_ref, g_ref, w1_ref, w2_ref, gf_ref, o_ref, h_scr, *, final_norm, row_chunk):
    f = pl.program_id(1)
    last = pl.num_programs(1) - 1
    tm = o_ref.shape[0]
    chunks = [slice(r, r + row_chunk) for r in range(0, tm, row_chunk)]

    def ff(h):
        u = jnp.maximum(jnp.dot(h, w1_ref[...], preferred_element_type=f32), 0.0)
        return jnp.dot((u * u).astype(bf16), w2_ref[...], preferred_element_type=f32)

    @pl.when(f == 0)
    def _():
        for rows in chunks:
            x = x_ref[rows, :]
            ms = jnp.mean(x * x, axis=-1, keepdims=True)
            h = ((x * lax.rsqrt(ms + EPS)) * g_ref[...]).astype(bf16)
            h_scr[rows, :] = h
            o_ref[rows, :] = x + ff(h)

    @pl.when(jnp.logical_and(f > 0, jnp.logical_or(f < last, not final_norm)))
    def _():
        o_ref[...] += ff(h_scr[...])

    if final_norm:
        @pl.when(jnp.logical_and(f > 0, f == last))
        def _():
            for rows in chunks:
                y = o_ref[rows, :] + ff(h_scr[rows, :])
                ms = jnp.mean(y * y, axis=-1, keepdims=True)
                o_ref[rows, :] = (y * lax.rsqrt(ms + EPS)) * gf_ref[...]


def _mlp(x2d, g, w1_bf, w2_bf, gf, *, final_norm, tm=1024, tf=1024, row_chunk=256):
    T, D = x2d.shape
    d_ff = w1_bf.shape[1]
    tm = min(tm, T)
    assert T % tm == 0 and d_ff % tf == 0 and d_ff // tf >= 2 and tm % row_chunk == 0
    return pl.pallas_call(
        functools.partial(_mlp_kernel, final_norm=final_norm, row_chunk=row_chunk),
        grid=(T // tm, d_ff // tf),
        in_specs=[
            pl.BlockSpec((tm, D), lambda i, f: (i, 0)),
            pl.BlockSpec((1, D), lambda i, f: (0, 0)),
            pl.BlockSpec((D, tf), lambda i, f: (0, f)),
            pl.BlockSpec((tf, D), lambda i, f: (f, 0)),
            pl.BlockSpec((1, D), lambda i, f: (0, 0)),
        ],
        out_specs=pl.BlockSpec((tm, D), lambda i, f: (i, 0)),
        out_shape=jax.ShapeDtypeStruct((T, D), f32),
        scratch_shapes=[pltpu.VMEM((tm, D), bf16)],
        compiler_params=pltpu.CompilerParams(
            dimension_semantics=("parallel", "arbitrary"), vmem_limit_bytes=VMEM_LIMIT),
        name="mlp",
    )(x2d, g, w1_bf, w2_bf, gf)


def kernel(x, g_attn, w_in, b_in, sinks_a, g_out_a, g_out_b, w_out, g_mlp, w_1, w_2, g_final):
    b, S, D = x.shape
    depth = w_in.shape[0]
    T = b * S
    bias_a = jnp.asarray(_band_bias(WINDOW_A - 1, 1))
    bias_b = jnp.asarray(np.stack([_band_bias(w // d, d) for w, d in DILATED_BRANCHES], axis=1))

    x2d = x.reshape(T, D)
    for l in range(depth):
        qa, kva, *qkv_b = _inproj(x2d, g_attn[l][None], w_in[l].astype(bf16), b_in[l][None], b)
        r3 = lambda t: t.reshape(b, S, t.shape[-1])
        ya = _attention_a(r3(qa), r3(kva), bias_a, sinks_a[l], g_out_a[l][None])
        ob = _attention_b(qkv_b, bias_b, b)
        x1 = _outproj(x2d, ya.reshape(T, Q_W), ob.reshape(T, Q_W), g_out_b[l][None],
                      w_out[l].astype(bf16))
        x2d = _mlp(x1, g_mlp[l][None], w_1[l].astype(bf16), w_2[l].astype(bf16), g_final[None],
                   final_norm=(l == depth - 1))
    return x2d.reshape(b, S, D)
```

```python
import functools

import numpy as np
import jax
import jax.numpy as jnp
from jax import lax
from jax.experimental import pallas as pl
from jax.experimental.pallas import tpu as pltpu

HEAD_DIM = 64
N_HEADS = 16
N_KV_A = 2
BLOCK = 128
WINDOW_A = 128
DILATED_BRANCHES = ((128, 1), (512, 4), (2048, 16))
EPS = 1e-5
NEG_INF = -1e30
LOG2E = 1.4426950408889634
Q_W = N_HEADS * HEAD_DIM
KV_A_W = N_KV_A * HEAD_DIM
LANES = 128
PACK = 16
VMEM_LIMIT = 60 * 1024 * 1024

bf16 = jnp.bfloat16
f32 = jnp.float32
_NT = (((1,), (1,)), ((), ()))


def _alibi_slopes(n):
    return 2.0 ** (-8.0 * (np.arange(n) + 1) / n)


def _band_bias(max_steps, step_dist):
    qi = np.arange(BLOCK)[:, None]
    kj = np.arange(2 * BLOCK)[None, :]
    steps = qi + BLOCK - kj
    valid = (steps >= 0) & (steps <= max_steps)
    slopes = _alibi_slopes(N_HEADS).astype(np.float32)
    pen = slopes[:, None, None] * (step_dist * steps).astype(np.float32)[None]
    pen = pen.astype(np.float64) * LOG2E
    gen =np.where(valid[None], -pen, np.float32(NEG_INF)).astype(np.float32)
    first = np.where((valid & (kj >= BLOCK))[None], -pen, np.float32(NEG_INF)).astype(np.float32)
    return np.stack([gen, first])


def _inproj_kernel(x_ref, g_ref, w_ref, b_ref, qa_ref, kva_ref, b1_ref, b4_ref, b16_ref,
                   ys, ys4, *, chunk, row_chunk):
    tm = x_ref.shape[0]
    qscale = HEAD_DIM ** -0.5 * LOG2E
    n4, n16 = row_chunk // 4, row_chunk // 16
    c0 = Q_W + 2 * KV_A_W
    slot = 0
    for part in range(tm // row_chunk):
        rows = slice(part * row_chunk, (part + 1) * row_chunk)
        x = x_ref[rows, :]
        ms = jnp.mean(x * x, axis=-1, keepdims=True)
        h = ((x * lax.rsqrt(ms + EPS)) * g_ref[...]).astype(bf16)

        def project(col, ch, scale):
            y = jnp.dot(h, w_ref[:, col:col + ch], preferred_element_type=f32)
            y = y + b_ref[:, col:col + ch]
            return y if scale is None else y * scale

        for cc in range(0, Q_W, chunk):
            qa_ref[rows, cc:cc + chunk] = project(cc, chunk, qscale).astype(bf16)
        kva_ref[rows, :] = project(Q_W, 2 * KV_A_W, None).astype(bf16)

        rows4 = slice(part * n4, (part + 1) * n4)
        rows16 = slice(part * n16, (part + 1) * n16)
        for cc in range(0, 3 * Q_W, chunk):
            y = project(c0 + cc, chunk, qscale if cc < Q_W else None)
            b1_ref[rows, cc:cc + chunk] = y.astype(bf16)
            for s in range(chunk // LANES):
                lanes = slice(cc + s * LANES, cc + (s + 1) * LANES)
                ys[slot, s] = y[:, s * LANES:(s + 1) * LANES]
                for r in range(4):
                    t = ys[slot, s, pl.ds(r, n4, stride=4), :]
                    b4_ref[0, r, rows4, lanes] = t.astype(bf16)
                    ys4[slot, s, r * n4:(r + 1) * n4, :] = t
                for r in range(4):
                    for c in range(4):
                        b16_ref[0, r + 4 * c, rows16, lanes] = ys4[
                            slot, s, pl.ds(r * n4 + c, n16, stride=4), :].astype(bf16)
            slot = 1 - slot


def _inproj(x2d, g, w_bf, b_in, batch, *, tm=512, chunk=512, row_chunk=512):
    T, D = x2d.shape
    S = T // batch
    d_in = w_bf.shape[1]
    assert 2 * KV_A_W + 4 * Q_W == d_in and S % tm == 0 and Q_W % chunk == 0
    tiles = S // tm
    const = lambda i: (0, 0)
    row = lambda i: (i, 0)
    res = lambda i: (i // tiles, 0, i % tiles, 0)
    return pl.pallas_call(
        functools.partial(_inproj_kernel, chunk=chunk, row_chunk=row_chunk),
        grid=(T // tm,),
        in_specs=[
            pl.BlockSpec((tm, D), row),
            pl.BlockSpec((1, D), const),
            pl.BlockSpec((D, d_in), const, pipeline_mode=pl.Buffered(1)),
            pl.BlockSpec((1, d_in), const),
        ],
        out_specs=[pl.BlockSpec((tm, Q_W), row),
                   pl.BlockSpec((tm, 2 * KV_A_W), row),
                   pl.BlockSpec((tm, 3 * Q_W), row),
                   pl.BlockSpec((1, 4, tm // 4, 3 * Q_W), res),
                   pl.BlockSpec((1, 16, tm // 16, 3 * Q_W), res)],
        out_shape=[jax.ShapeDtypeStruct((T, Q_W), bf16),
                   jax.ShapeDtypeStruct((T, 2 * KV_A_W), bf16),
                   jax.ShapeDtypeStruct((T, 3 * Q_W), bf16),
                   jax.ShapeDtypeStruct((batch, 4, S // 4, 3 * Q_W), bf16),
                   jax.ShapeDtypeStruct((batch, 16, S // 16, 3 * Q_W), bf16)],
        scratch_shapes=[pltpu.VMEM((2, chunk // LANES, row_chunk, LANES), f32),
                        pltpu.VMEM((2, chunk // LANES, row_chunk, LANES), f32)],
        compiler_params=pltpu.CompilerParams(
            dimension_semantics=("parallel",), vmem_limit_bytes=VMEM_LIMIT),
        name="inproj",
    )(x2d, g, w_bf, b_in)


def _attn_a_kernel(q_ref, k_ref, kp_ref, v_ref, vp_ref, bias_ref, sink_ref, g_ref, out_ref,
                   k_scr, v_scr, o_scr, *, tq):
    li = pl.program_id(1)
    lane = lax.broadcasted_iota(jnp.int32, (1, LANES), 1)
    lo = lane < HEAD_DIM

    def dup(t):
        r = pltpu.roll(t.astype(f32), HEAD_DIM, 1).astype(bf16)
        return jnp.concatenate([jnp.where(lo, t, r), jnp.where(lo, r, t)], axis=1)
    k_scr[0:BLOCK, :] = dup(kp_ref[0])
    k_scr[BLOCK:, :] = dup(k_ref[0])
    v_scr[0:BLOCK, :] = dup(vp_ref[0])
    v_scr[BLOCK:, :] = dup(v_ref[0])

    first = jnp.where(li == 0, 1, 0)
    ones = jnp.ones((2 * BLOCK, LANES), bf16)
    lane0 = lane == 0
    row0 = lax.broadcasted_iota(jnp.int32, (PACK, 1), 0) == 0
    assert WINDOW_A <= BLOCK
    for j in range(tq // BLOCK):
        rows = slice(j * BLOCK, (j + 1) * BLOCK)
        krows = slice(j * BLOCK, (j + 2) * BLOCK)
        sel = first if j == 0 else 0
        for hp in range(N_HEADS // 2):
            cols = slice(hp * LANES, (hp + 1) * LANES)
            g = (2 * hp) // (N_HEADS // N_KV_A)
            kcols = slice(g * LANES, (g + 1) * LANES)
            q2 = q_ref[0, rows, cols]
            k2 = k_scr[krows, kcols]
            v_top = jnp.where(row0, jnp.zeros((PACK, LANES), bf16),
                              v_scr[j * BLOCK:j * BLOCK + PACK, kcols])
            v2 = jnp.concatenate([v_top, v_scr[j * BLOCK + PACK:(j + 2) * BLOCK, kcols]], axis=0)
            v2a = jnp.concatenate([v2, ones], axis=1)
            zero = jnp.zeros_like(q2)
            pv, l = [], []
            for e in range(2):
                h = 2 * hp + e
                qm = jnp.where(lo, q2, zero) if e == 0 else jnp.where(lo, zero, q2)
                s = lax.dot_general(qm, k2, _NT, preferred_element_type=f32)
                s_prev = jnp.where(lane0, sink_ref[h] * LOG2E,
                                   s[:, :BLOCK] + bias_ref[sel, h, :, :BLOCK])
                s = jnp.concatenate([s_prev, s[:, BLOCK:] + bias_ref[sel, h, :, BLOCK:]], axis=1)
                p = jnp.exp2(s - jnp.max(s, axis=-1, keepdims=True))
                pva = jnp.dot(p.astype(bf16), v2a, preferred_element_type=f32)
                pv.append(pva[:, :LANES])
                l.append(pva[:, LANES:])
            o_scr[rows, cols] = jnp.where(lo, pv[0], pv[1]) / jnp.where(lo, l[0], l[1])

    o = o_scr[...]
    ms = jnp.mean(o * o, axis=-1, keepdims=True)
    out_ref[0] = ((o * lax.rsqrt(ms + EPS)) * g_ref[...]).astype(bf16)


def _attention_a(q, kv, bias, sinks, gain, *, tq=512):
    b, S, _ = q.shape
    tq = min(tq, S)
    assert S % tq == 0 and tq % BLOCK == 0
    nsub = tq // BLOCK
    cur = lambda bi, li: (bi, li, 0)
    k, v = kv, kv
    return pl.pallas_call(
        functools.partial(_attn_a_kernel, tq=tq),
        grid=(b, S // tq),
        in_specs=[
            pl.BlockSpec((1, tq, Q_W), cur),
            pl.BlockSpec((1, tq, KV_A_W), lambda bi, li: (bi, li, 0)),
            pl.BlockSpec((1, BLOCK, KV_A_W), lambda bi, li: (bi, jnp.maximum(li * nsub - 1, 0), 0)),
            pl.BlockSpec((1, tq, KV_A_W), lambda bi, li: (bi, li, 1)),
            pl.BlockSpec((1, BLOCK, KV_A_W), lambda bi, li: (bi, jnp.maximum(li * nsub - 1, 0), 1)),
            pl.BlockSpec(bias.shape, lambda bi, li: (0, 0, 0, 0)),
            pl.BlockSpec(memory_space=pltpu.SMEM),
            pl.BlockSpec((1, Q_W), lambda bi, li: (0, 0)),
        ],
        out_specs=pl.BlockSpec((1, tq, Q_W), cur),
        out_shape=jax.ShapeDtypeStruct((b, S, Q_W), bf16),
        scratch_shapes=[pltpu.VMEM((tq + BLOCK, 2 * KV_A_W), bf16),
                        pltpu.VMEM((tq + BLOCK, 2 * KV_A_W), bf16),
                        pltpu.VMEM((tq, Q_W), f32)],
        compiler_params=pltpu.CompilerParams(
            dimension_semantics=("parallel", "arbitrary"), vmem_limit_bytes=VMEM_LIMIT),
        name="attn_a",
    )(q, k, k, v, v, bias, sinks, gain)


B_TILE = 2048
B_DILS = tuple(d for _, d in DILATED_BRANCHES)
assert B_DILS == (1, 4, 16) and all(w // d == BLOCK for w, d in DILATED_BRANCHES)
B_NSUB = B_TILE // BLOCK


def _attn_b_kernel(q1, q4, q16, k1, k1h, k4, k4h, k16, k16h, v1, v1h, v4, v4h, v16, v16h,
                   bias_ref, o_ref, pv_st, l_st, m_st):
    first = jnp.where(pl.program_id(2) == 0, 1, 0)
    lane = lax.broadcasted_iota(jnp.int32, (1, LANES), 1)
    lo = lane < HEAD_DIM
    ones = jnp.ones((2 * BLOCK, LANES), bf16)

    def pair_block(br, q2, k2, v2, sel):
        v2a = jnp.concatenate([v2, ones], axis=1)
        zero = jnp.zeros_like(q2)
        pv, l, m = [], [], []
        for e in range(2):
            qm = jnp.where(lo, q2, zero) if e == 0 else jnp.where(lo, zero, q2)
            s = lax.dot_general(qm, k2, _NT, preferred_element_type=f32)
            s = s + bias_ref[sel, br, e]
            me = jnp.max(s, axis=-1, keepdims=True)
            p = jnp.exp2(s - me)
            pva = jnp.dot(p.astype(bf16), v2a, preferred_element_type=f32)
            pv.append(pva[:, :LANES])
            l.append(pva[:, LANES:])
            m.append(jnp.broadcast_to(me, (BLOCK, LANES)))
        return (jnp.where(lo, pv[0], pv[1]), jnp.where(lo, l[0], l[1]),
                jnp.where(lo, m[0], m[1]))

    def window(cur, halo, j):
        if j == 0:
            return jnp.concatenate([halo, cur[0:BLOCK, :]], axis=0)
        return cur[(j - 1) * BLOCK:(j + 1) * BLOCK, :]

    def keep(slot, rows, res):
        pv_st[slot, rows, :], l_st[slot, rows, :], m_st[slot, rows, :] = res

    n4 = B_TILE // 4
    for r in range(4):
        for c in range(4):
            keep(2, pl.ds(r * n4 + c, BLOCK, stride=4),
                 pair_block(2, q16[0, r + 4 * c],
                            window(k16.at[0, r + 4 * c], k16h[0, r + 4 * c], 0),
                            window(v16.at[0, r + 4 * c], v16h[0, r + 4 * c], 0), first))
        for j in range(n4 // BLOCK):
            mid = slice(r * n4 + j * BLOCK, r * n4 + (j + 1) * BLOCK)
            keep(1, pl.ds(j * 4 * BLOCK + r, BLOCK, stride=4),
                 (pv_st[2, mid, :], l_st[2, mid, :], m_st[2, mid, :]))
    for r in range(4):
        for j in range(B_NSUB // 4):
            keep(0, pl.ds(j * 4 * BLOCK + r, BLOCK, stride=4),
                 pair_block(1, q4[0, r, j * BLOCK:(j + 1) * BLOCK, :],
                            window(k4.at[0, r], k4h[0, r], j), window(v4.at[0, r], v4h[0, r], j),
                            first if j == 0 else 0))
    for j in range(B_NSUB):
        rows = slice(j * BLOCK, (j + 1) * BLOCK)
        pvs, ls, ms = pair_block(0, q1[0, rows, :], window(k1.at[0], k1h[0], j),
                                 window(v1.at[0], v1h[0], j), first if j == 0 else 0)
        pvs, ls, ms = [pvs], [ls], [ms]
        for slot in range(2):
            pvs.append(pv_st[slot, rows, :])
            ls.append(l_st[slot, rows, :])
            ms.append(m_st[slot, rows, :])
        mx = jnp.maximum(jnp.maximum(ms[0], ms[1]), ms[2])
        num = den = None
        for n in range(3):
            w = jnp.exp2(ms[n] - mx)
            num = w * pvs[n] if num is None else num + w * pvs[n]
            den = w * ls[n] if den is None else den + w * ls[n]
        o_ref[0, rows, :] = num / den


def _attention_b(qkv, bias, batch):
    b1, b4, b16 = qkv
    S = b1.shape[0] // batch
    assert S % B_TILE == 0
    b1 = b1.reshape(batch, S, 3 * Q_W)
    npair = N_HEADS // 2
    n4, n16 = B_TILE // 4, B_TILE // 16
    assert n16 == BLOCK

    def specs(part):
        col = lambda hp: part * npair + hp
        nat = pl.BlockSpec((1, B_TILE, LANES), lambda bi, hp, ti: (bi, ti, col(hp)))
        nat_h = pl.BlockSpec((1, BLOCK, LANES), lambda bi, hp, ti: (
            bi, jnp.maximum(ti * B_NSUB - 1, 0), col(hp)))
        d4 = pl.BlockSpec((1, 4, n4, LANES), lambda bi, hp, ti: (bi, 0, ti, col(hp)))
        d4_h = pl.BlockSpec((1, 4, BLOCK, LANES), lambda bi, hp, ti: (
            bi, 0, jnp.maximum(ti * (n4 // BLOCK) - 1, 0), col(hp)))
        d16 = pl.BlockSpec((1, 16, n16, LANES), lambda bi, hp, ti: (bi, 0, ti, col(hp)))
        d16_h = pl.BlockSpec((1, 16, BLOCK, LANES), lambda bi, hp, ti: (
            bi, 0, jnp.maximum(ti - 1, 0), col(hp)))
        return nat, nat_h, d4, d4_h, d16, d16_h

    q_nat, _, q_d4, _, q_d16, _ = specs(0)
    kv_args = (b1, b1, b4, b4, b16, b16)
    return pl.pallas_call(
        _attn_b_kernel,
        grid=(batch, npair, S // B_TILE),
        in_specs=[q_nat, q_d4, q_d16, *specs(1), *specs(2),
                  pl.BlockSpec((2, 3, 2, BLOCK, 2 * BLOCK), lambda bi, hp, ti: (0, 0, hp, 0, 0))],
        out_specs=pl.BlockSpec((1, B_TILE, LANES), lambda bi, hp, ti: (bi, ti, hp)),
        out_shape=jax.ShapeDtypeStruct((batch, S, Q_W), f32),
        scratch_shapes=[pltpu.VMEM((3, B_TILE, LANES), f32),
                        pltpu.VMEM((3, B_TILE, LANES), f32),
                        pltpu.VMEM((3, B_TILE, LANES), f32)],
        compiler_params=pltpu.CompilerParams(
            dimension_semantics=("parallel", "parallel", "arbitrary"),
            vmem_limit_bytes=VMEM_LIMIT),
        name="attn_b",
    )(b1, b4, b16, *kv_args, *kv_args, bias)


def _outproj_kernel(x_ref, ya_ref, ob_ref, g_ref, w_ref, o_ref):
    ob = ob_ref[...]
    ms = jnp.mean(ob * ob, axis=-1, keepdims=True)
    yb = ((ob * lax.rsqrt(ms + EPS)) * g_ref[...]).astype(bf16)
    acc = jnp.dot(ya_ref[...], w_ref[0:Q_W, :], preferred_element_type=f32)
    acc = acc + jnp.dot(yb, w_ref[Q_W:, :], preferred_element_type=f32)
    o_ref[...] = x_ref[...] + acc


def _outproj(x2d, ya, ob, g_b, w_bf, *, tm=512):
    T, D = x2d.shape
    return pl.pallas_call(
        _outproj_kernel,
        grid=(T // tm,),
        in_specs=[
            pl.BlockSpec((tm, D), lambda i: (i, 0)),
            pl.BlockSpec((tm, Q_W), lambda i: (i, 0)),
            pl.BlockSpec((tm, Q_W), lambda i: (i, 0)),
            pl.BlockSpec((1, Q_W), lambda i: (0, 0)),
            pl.BlockSpec(w_bf.shape, lambda i: (0, 0), pipeline_mode=pl.Buffered(1)),
        ],
        out_specs=pl.BlockSpec((tm, D), lambda i: (i, 0)),
        out_shape=jax.ShapeDtypeStruct((T, D), f32),
        compiler_params=pltpu.CompilerParams(
            dimension_semantics=("parallel",), vmem_limit_bytes=VMEM_LIMIT),
        name="outproj",
    )(x2d, ya, ob, g_b, w_bf)


def _mlp_kernel(x_ref, g_ref, w1_ref, w2_ref, gf_ref, o_ref, h_scr, *, final_norm, row_chunk):
    f = pl.program_id(1)
    last = pl.num_programs(1) - 1
    tm = o_ref.shape[0]
    chunks = [slice(r, r + row_chunk) for r in range(0, tm, row_chunk)]

    def ff(h):
        u = jnp.maximum(jnp.dot(h, w1_ref[...], preferred_element_type=f32), 0.0)
        return jnp.dot((u * u).astype(bf16), w2_ref[...], preferred_element_type=f32)

    @pl.when(f == 0)
    def _():
        for rows in chunks:
            x = x_ref[rows, :]
            ms = jnp.mean(x * x, axis=-1, keepdims=True)
            h = ((x * lax.rsqrt(ms + EPS)) * g_ref[...]).astype(bf16)
            h_scr[rows, :] = h
            o_ref[rows, :] = x + ff(h)

    @pl.when(jnp.logical_and(f > 0, jnp.logical_or(f < last, not final_norm)))
    def _():
        o_ref[...] += ff(h_scr[...])

    if final_norm:
        @pl.when(jnp.logical_and(f > 0, f == last))
        def _():
            for rows in chunks:
                y = o_ref[rows, :] + ff(h_scr[rows, :])
                ms = jnp.mean(y * y, axis=-1, keepdims=True)
                o_ref[rows, :] = (y * lax.rsqrt(ms + EPS)) * gf_ref[...]


def _mlp(x2d, g, w1_bf, w2_bf, gf, *, final_norm, tm=1024, tf=1024, row_chunk=256):
    T, D = x2d.shape
    d_ff = w1_bf.shape[1]
    tm = min(tm, T)
    assert T % tm == 0 and d_ff % tf == 0 and d_ff // tf >= 2 and tm % row_chunk == 0
    return pl.pallas_call(
        functools.partial(_mlp_kernel, final_norm=final_norm, row_chunk=row_chunk),
        grid=(T // tm, d_ff // tf),
        in_specs=[
            pl.BlockSpec((tm, D), lambda i, f: (i, 0)),
            pl.BlockSpec((1, D), lambda i, f: (0, 0)),
            pl.BlockSpec((D, tf), lambda i, f: (0, f)),
            pl.BlockSpec((tf, D), lambda i, f: (f, 0)),
            pl.BlockSpec((1, D), lambda i, f: (0, 0)),
        ],
        out_specs=pl.BlockSpec((tm, D), lambda i, f: (i, 0)),
        out_shape=jax.ShapeDtypeStruct((T, D), f32),
        scratch_shapes=[pltpu.VMEM((tm, D), bf16)],
        compiler_params=pltpu.CompilerParams(
            dimension_semantics=("parallel", "arbitrary"), vmem_limit_bytes=VMEM_LIMIT),
        name="mlp",
    )(x2d, g, w1_bf, w2_bf, gf)


def kernel(x, g_attn, w_in, b_in, sinks_a, g_out_a, g_out_b, w_out, g_mlp, w_1, w_2, g_final):
    b, S, D = x.shape
    depth = w_in.shape[0]
    T = b * S
    bias_a = jnp.asarray(_band_bias(WINDOW_A - 1, 1))
    bias_b = jnp.asarray(np.stack([_band_bias(w // d, d) for w, d in DILATED_BRANCHES], axis=1))

    x2d = x.reshape(T, D)
    for l in range(depth):
        qa, kva, *qkv_b = _inproj(x2d, g_attn[l][None], w_in[l].astype(bf16), b_in[l][None], b)
        r3 = lambda t: t.reshape(b, S, t.shape[-1])
        ya = _attention_a(r3(qa), r3(kva), bias_a, sinks_a[l], g_out_a[l][None])
        ob = _attention_b(qkv_b, bias_b, b)
        x1 = _outproj(x2d, ya.reshape(T, Q_W), ob.reshape(T, Q_W), g_out_b[l][None],
                      w_out[l].astype(bf16))
        x2d = _mlp(x1, g_mlp[l][None], w_1[l].astype(bf16), w_2[l].astype(bf16), g_final[None],
                   final_norm=(l == depth - 1))
    return x2d.reshape(b, S, D)
```

```python
import functools

import numpy as np
import jax
import jax.numpy as jnp
from jax import lax
from jax.experimental import pallas as pl
from jax.experimental.pallas import tpu as pltpu

HEAD_DIM = 64
N_HEADS = 16
N_KV_A = 2
BLOCK = 128
WINDOW_A = 128
DILATED_BRANCHES = ((128, 1), (512, 4), (2048, 16))
EPS = 1e-5
NEG_INF = -1e30
LOG2E = 1.4426950408889634
Q_W = N_HEADS * HEAD_DIM
KV_A_W = N_KV_A * HEAD_DIM
LANES = 128
PACK = 16
VMEM_LIMIT = 60 * 1024 * 1024

bf16 = jnp.bfloat16
f32 = jnp.float32
_NT = (((1,), (1,)), ((), ()))


def _alibi_slopes(n):
    return 2.0 ** (-8.0 * (np.arange(n) + 1) / n)


def _band_bias(max_steps, step_dist):
    qi = np.arange(BLOCK)[:, None]
    kj = np.arange(2 * BLOCK)[None, :]
    steps = qi + BLOCK - kj
    valid = (steps >= 0) & (steps <= max_steps)
    slopes = _alibi_slopes(N_HEADS).astype(np.float32)
    pen = slopes[:, None, None] * (step_dist * steps).astype(np.float32)[None]
    pen = pen.astype(np.float64) * LOG2E
    gen =np.where(valid[None], -pen, np.float32(NEG_INF)).astype(np.float32)
    first = np.where((valid & (kj >= BLOCK))[None], -pen, np.float32(NEG_INF)).astype(np.float32)
    return np.stack([gen, first])


def _inproj_kernel(x_ref, g_ref, w_ref, b_ref, qa_ref, kva_ref, b1_ref, b4_ref, b16_ref,
                   ys, ys4, *, chunk, row_chunk):
    tm = x_ref.shape[0]
    qscale = HEAD_DIM ** -0.5 * LOG2E
    n4, n16 = row_chunk // 4, row_chunk // 16
    c0 = Q_W + 2 * KV_A_W
    slot = 0
    for part in range(tm // row_chunk):
        rows = slice(part * row_chunk, (part + 1) * row_chunk)
        x = x_ref[rows, :]
        ms = jnp.mean(x * x, axis=-1, keepdims=True)
        h = ((x * lax.rsqrt(ms + EPS)) * g_ref[...]).astype(bf16)

        def project(col, ch, scale):
            y = jnp.dot(h, w_ref[:, col:col + ch], preferred_element_type=f32)
            y = y + b_ref[:, col:col + ch]
            return y if scale is None else y * scale

        for cc in range(0, Q_W, chunk):
            qa_ref[rows, cc:cc + chunk] = project(cc, chunk, qscale).astype(bf16)
        kva_ref[rows, :] = project(Q_W, 2 * KV_A_W, None).astype(bf16)

        rows4 = slice(part * n4, (part + 1) * n4)
        rows16 = slice(part * n16, (part + 1) * n16)
        for cc in range(0, 3 * Q_W, chunk):
            y = project(c0 + cc, chunk, qscale if cc < Q_W else None)
            b1_ref[rows, cc:cc + chunk] = y.astype(bf16)
            for s in range(chunk // LANES):
                lanes = slice(cc + s * LANES, cc + (s + 1) * LANES)
                ys[slot, s] = y[:, s * LANES:(s + 1) * LANES]
                for r in range(4):
                    t = ys[slot, s, pl.ds(r, n4, stride=4), :]
                    b4_ref[0, r, rows4, lanes] = t.astype(bf16)
                    ys4[slot, s, r * n4:(r + 1) * n4, :] = t
                for r in range(4):
                    for c in range(4):
                        b16_ref[0, r + 4 * c, rows16, lanes] = ys4[
                            slot, s, pl.ds(r * n4 + c, n16, stride=4), :].astype(bf16)
            slot = 1 - slot


def _inproj(x2d, g, w_bf, b_in, batch, *, tm=512, chunk=512, row_chunk=512):
    T, D = x2d.shape
    S = T // batch
    d_in = w_bf.shape[1]
    assert 2 * KV_A_W + 4 * Q_W == d_in and S % tm == 0 and Q_W % chunk == 0
    tiles = S // tm
    const = lambda i: (0, 0)
    row = lambda i: (i, 0)
    res = lambda i: (i // tiles, 0, i % tiles, 0)
    return pl.pallas_call(
        functools.partial(_inproj_kernel, chunk=chunk, row_chunk=row_chunk),
        grid=(T // tm,),
        in_specs=[
            pl.BlockSpec((tm, D), row),
            pl.BlockSpec((1, D), const),
            pl.BlockSpec((D, d_in), const, pipeline_mode=pl.Buffered(1)),
            pl.BlockSpec((1, d_in), const),
        ],
        out_specs=[pl.BlockSpec((tm, Q_W), row),
                   pl.BlockSpec((tm, 2 * KV_A_W), row),
                   pl.BlockSpec((tm, 3 * Q_W), row),
                   pl.BlockSpec((1, 4, tm // 4, 3 * Q_W), res),
                   pl.BlockSpec((1, 16, tm // 16, 3 * Q_W), res)],
        out_shape=[jax.ShapeDtypeStruct((T, Q_W), bf16),
                   jax.ShapeDtypeStruct((T, 2 * KV_A_W), bf16),
                   jax.ShapeDtypeStruct((T, 3 * Q_W), bf16),
                   jax.ShapeDtypeStruct((batch, 4, S // 4, 3 * Q_W), bf16),
                   jax.ShapeDtypeStruct((batch, 16, S // 16, 3 * Q_W), bf16)],
        scratch_shapes=[pltpu.VMEM((2, chunk // LANES, row_chunk, LANES), f32),
                        pltpu.VMEM((2, chunk // LANES, row_chunk, LANES), f32)],
        compiler_params=pltpu.CompilerParams(
            dimension_semantics=("parallel",), vmem_limit_bytes=VMEM_LIMIT),
        name="inproj",
    )(x2d, g, w_bf, b_in)


def _attn_a_kernel(q_ref, k_ref, kp_ref, v_ref, vp_ref, bias_ref, sink_ref, g_ref, out_ref,
                   k_scr, v_scr, o_scr, *, tq):
    li = pl.program_id(1)
    lane = lax.broadcasted_iota(jnp.int32, (1, LANES), 1)
    lo = lane < HEAD_DIM

    def dup(t):
        r = pltpu.roll(t.astype(f32), HEAD_DIM, 1).astype(bf16)
        return jnp.concatenate([jnp.where(lo, t, r), jnp.where(lo, r, t)], axis=1)
    k_scr[0:BLOCK, :] = dup(kp_ref[0])
    k_scr[BLOCK:, :] = dup(k_ref[0])
    v_scr[0:BLOCK, :] = dup(vp_ref[0])
    v_scr[BLOCK:, :] = dup(v_ref[0])

    first = jnp.where(li == 0, 1, 0)
    ones = jnp.ones((2 * BLOCK, LANES), bf16)
    lane0 = lane == 0
    row0 = lax.broadcasted_iota(jnp.int32, (PACK, 1), 0) == 0
    assert WINDOW_A <= BLOCK
    for j in range(tq // BLOCK):
        rows = slice(j * BLOCK, (j + 1) * BLOCK)
        krows = slice(j * BLOCK, (j + 2) * BLOCK)
        sel = first if j == 0 else 0
        for hp in range(N_HEADS // 2):
            cols = slice(hp * LANES, (hp + 1) * LANES)
            g = (2 * hp) // (N_HEADS // N_KV_A)
            kcols = slice(g * LANES, (g + 1) * LANES)
            q2 = q_ref[0, rows, cols]
            k2 = k_scr[krows, kcols]
            v_top = jnp.where(row0, jnp.zeros((PACK, LANES), bf16),
                              v_scr[j * BLOCK:j * BLOCK + PACK, kcols])
            v2 = jnp.concatenate([v_top, v_scr[j * BLOCK + PACK:(j + 2) * BLOCK, kcols]], axis=0)
            v2a = jnp.concatenate([v2, ones], axis=1)
            zero = jnp.zeros_like(q2)
            pv, l = [], []
            for e in range(2):
                h = 2 * hp + e
                qm = jnp.where(lo, q2, zero) if e == 0 else jnp.where(lo, zero, q2)
                s = lax.dot_general(qm, k2, _NT, preferred_element_type=f32)
                s_prev = jnp.where(lane0, sink_ref[h] * LOG2E,
                                   s[:, :BLOCK] + bias_ref[sel, h, :, :BLOCK])
                s = jnp.concatenate([s_prev, s[:, BLOCK:] + bias_ref[sel, h, :, BLOCK:]], axis=1)
                p = jnp.exp2(s - jnp.max(s, axis=-1, keepdims=True))
                pva = jnp.dot(p.astype(bf16), v2a, preferred_element_type=f32)
                pv.append(pva[:, :LANES])
                l.append(pva[:, LANES:])
            o_scr[rows, cols] = jnp.where(lo, pv[0], pv[1]) / jnp.where(lo, l[0], l[1])

    o = o_scr[...]
    ms = jnp.mean(o * o, axis=-1, keepdims=True)
    out_ref[0] = ((o * lax.rsqrt(ms + EPS)) * g_ref[...]).astype(bf16)


def _attention_a(q, kv, bias, sinks, gain, *, tq=1024):
    b, S, _ = q.shape
    tq = min(tq, S)
    assert S % tq == 0 and tq % BLOCK == 0
    nsub = tq // BLOCK
    cur = lambda bi, li: (bi, li, 0)
    k, v = kv, kv
    return pl.pallas_call(
        functools.partial(_attn_a_kernel, tq=tq),
        grid=(b, S // tq),
        in_specs=[
            pl.BlockSpec((1, tq, Q_W), cur),
            pl.BlockSpec((1, tq, KV_A_W), lambda bi, li: (bi, li, 0)),
            pl.BlockSpec((1, BLOCK, KV_A_W), lambda bi, li: (bi, jnp.maximum(li * nsub - 1, 0), 0)),
            pl.BlockSpec((1, tq, KV_A_W), lambda bi, li: (bi, li, 1)),
            pl.BlockSpec((1, BLOCK, KV_A_W), lambda bi, li: (bi, jnp.maximum(li * nsub - 1, 0), 1)),
            pl.BlockSpec(bias.shape, lambda bi, li: (0, 0, 0, 0)),
            pl.BlockSpec(memory_space=pltpu.SMEM),
            pl.BlockSpec((1, Q_W), lambda bi, li: (0, 0)),
        ],
        out_specs=pl.BlockSpec((1, tq, Q_W), cur),
        out_shape=jax.ShapeDtypeStruct((b, S, Q_W), bf16),
        scratch_shapes=[pltpu.VMEM((tq + BLOCK, 2 * KV_A_W), bf16),
                        pltpu.VMEM((tq + BLOCK, 2 * KV_A_W), bf16),
                        pltpu.VMEM((tq, Q_W), f32)],
        compiler_params=pltpu.CompilerParams(
            dimension_semantics=("parallel", "arbitrary"), vmem_limit_bytes=VMEM_LIMIT),
        name="attn_a",
    )(q, k, k, v, v, bias, sinks, gain)


B_TILE = 2048
B_DILS = tuple(d for _, d in DILATED_BRANCHES)
assert B_DILS == (1, 4, 16) and all(w // d == BLOCK for w, d in DILATED_BRANCHES)
B_NSUB = B_TILE // BLOCK
B_PAIRS = 2


def _attn_b_kernel(q1, q4, q16, k1, k1h, k4, k4h, k16, k16h, v1, v1h, v4, v4h, v16, v16h,
                   bias_ref, o_ref, pv_st, l_st, m_st):
    first = jnp.where(pl.program_id(2) == 0, 1, 0)
    lane = lax.broadcasted_iota(jnp.int32, (1, LANES), 1)
    lo = lane < HEAD_DIM
    ones = jnp.ones((2 * BLOCK, LANES), bf16)

    def pair_block(br, h0, q2, k2, v2, sel):
        v2a = jnp.concatenate([v2, ones], axis=1)
        zero = jnp.zeros_like(q2)
        pv, l, m = [], [], []
        for e in range(2):
            qm = jnp.where(lo, q2, zero) if e == 0 else jnp.where(lo, zero, q2)
            s = lax.dot_general(qm, k2, _NT, preferred_element_type=f32)
            s = s + bias_ref[sel, br, h0 + e]
            me = jnp.max(s, axis=-1, keepdims=True)
            p = jnp.exp2(s - me)
            pva = jnp.dot(p.astype(bf16), v2a, preferred_element_type=f32)
            pv.append(pva[:, :LANES])
            l.append(pva[:, LANES:])
            m.append(jnp.broadcast_to(me, (BLOCK, LANES)))
        return (jnp.where(lo, pv[0], pv[1]), jnp.where(lo, l[0], l[1]),
                jnp.where(lo, m[0], m[1]))

    def window(cur, halo, j, lanes):
        if j == 0:
            return jnp.concatenate([halo[:, lanes], cur[0:BLOCK, lanes]], axis=0)
        return cur[(j - 1) * BLOCK:(j + 1) * BLOCK, lanes]

    n4 = B_TILE // 4
    for pair in range(B_PAIRS):
        lanes = slice(pair * LANES, (pair + 1) * LANES)

        def keep(slot, rows, res):
            s = slot * B_PAIRS + pair
            pv_st[s, rows, :], l_st[s, rows, :], m_st[s, rows, :] = res

        def kept(slot, rows):
            s = slot * B_PAIRS + pair
            return pv_st[s, rows, :], l_st[s, rows, :], m_st[s, rows, :]

        def block(br, q2, k_cur, k_halo, v_cur, v_halo, j):
            return pair_block(br, 2 * pair, q2, window(k_cur, k_halo, j, lanes),
                              window(v_cur, v_halo, j, lanes), first if j == 0 else 0)

        for r in range(4):
            for c in range(4):
                r16 = r + 4 * c
                keep(2, pl.ds(r * n4 + c, BLOCK, stride=4),
                     block(2, q16[0, r16, :, lanes], k16.at[0, r16], k16h.at[0, r16],
                           v16.at[0, r16], v16h.at[0, r16], 0))
            for j in range(n4 // BLOCK):
                keep(1, pl.ds(j * 4 * BLOCK + r, BLOCK, stride=4),
                     kept(2, slice(r * n4 + j * BLOCK, r * n4 + (j + 1) * BLOCK)))
        for r in range(4):
            for j in range(B_NSUB // 4):
                keep(0, pl.ds(j * 4 * BLOCK + r, BLOCK, stride=4),
                     block(1, q4[0, r, j * BLOCK:(j + 1) * BLOCK, lanes], k4.at[0, r],
                           k4h.at[0, r], v4.at[0, r], v4h.at[0, r], j))
        for j in range(B_NSUB):
            rows = slice(j * BLOCK, (j + 1) * BLOCK)
            parts = [block(0, q1[0, rows, lanes], k1.at[0], k1h.at[0], v1.at[0], v1h.at[0], j),
                     kept(0, rows), kept(1, rows)]
            mx = jnp.maximum(jnp.maximum(parts[0][2], parts[1][2]), parts[2][2])
            num = den = None
            for pv_n, l_n, m_n in parts:
                w = jnp.exp2(m_n - mx)
                num = w * pv_n if num is None else num + w * pv_n
                den = w * l_n if den is None else den + w * l_n
            o_ref[0, rows, lanes] = num / den


def _attention_b(qkv, bias, batch):
    b1, b4, b16 = qkv
    S = b1.shape[0] // batch
    assert S % B_TILE == 0
    b1 = b1.reshape(batch, S, 3 * Q_W)
    nstep = N_HEADS // (2 * B_PAIRS)
    W = B_PAIRS * LANES
    n4, n16 = B_TILE // 4, B_TILE // 16
    assert n16 == BLOCK

    def specs(part):
        col = lambda hg: part * nstep + hg
        nat = pl.BlockSpec((1, B_TILE, W), lambda bi, hg, ti: (bi, ti, col(hg)))
        nat_h = pl.BlockSpec((1, BLOCK, W), lambda bi, hg, ti: (
            bi, jnp.maximum(ti * B_NSUB - 1, 0), col(hg)))
        d4 = pl.BlockSpec((1, 4, n4, W), lambda bi, hg, ti: (bi, 0, ti, col(hg)))
        d4_h = pl.BlockSpec((1, 4, BLOCK, W), lambda bi, hg, ti: (
            bi, 0, jnp.maximum(ti * (n4 // BLOCK) - 1, 0), col(hg)))
        d16 = pl.BlockSpec((1, 16, n16, W), lambda bi, hg, ti: (bi, 0, ti, col(hg)))
        d16_h = pl.BlockSpec((1, 16, BLOCK, W), lambda bi, hg, ti: (
            bi, 0, jnp.maximum(ti - 1, 0), col(hg)))
        return nat, nat_h, d4, d4_h, d16, d16_h

    q_nat, _, q_d4, _, q_d16, _ = specs(0)
    kv_args = (b1, b1, b4, b4, b16, b16)
    state = pltpu.VMEM((3 * B_PAIRS, B_TILE, LANES), f32)
    return pl.pallas_call(
        _attn_b_kernel,
        grid=(batch, nstep, S // B_TILE),
        in_specs=[q_nat, q_d4, q_d16, *specs(1), *specs(2),
                  pl.BlockSpec((2, 3, 2 * B_PAIRS, BLOCK, 2 * BLOCK),
                               lambda bi, hg, ti: (0, 0, hg, 0, 0))],
        out_specs=pl.BlockSpec((1, B_TILE, W), lambda bi, hg, ti: (bi, ti, hg)),
        out_shape=jax.ShapeDtypeStruct((batch, S, Q_W), f32),
        scratch_shapes=[state, state, state],
        compiler_params=pltpu.CompilerParams(
            dimension_semantics=("parallel", "parallel", "arbitrary"),
            vmem_limit_bytes=VMEM_LIMIT),
        name="attn_b",
    )(b1, b4, b16, *kv_args, *kv_args, bias)


def _outproj_kernel(x_ref, ya_ref, ob_ref, g_ref, w_ref, o_ref):
    ob = ob_ref[...]
    ms = jnp.mean(ob * ob, axis=-1, keepdims=True)
    yb = ((ob * lax.rsqrt(ms + EPS)) * g_ref[...]).astype(bf16)
    acc = jnp.dot(ya_ref[...], w_ref[0:Q_W, :], preferred_element_type=f32)
    acc = acc + jnp.dot(yb, w_ref[Q_W:, :], preferred_element_type=f32)
    o_ref[...] = x_ref[...] + acc


def _outproj(x2d, ya, ob, g_b, w_bf, *, tm=512):
    T, D = x2d.shape
    return pl.pallas_call(
        _outproj_kernel,
        grid=(T // tm,),
        in_specs=[
            pl.BlockSpec((tm, D), lambda i: (i, 0)),
            pl.BlockSpec((tm, Q_W), lambda i: (i, 0)),
            pl.BlockSpec((tm, Q_W), lambda i: (i, 0)),
            pl.BlockSpec((1, Q_W), lambda i: (0, 0)),
            pl.BlockSpec(w_bf.shape, lambda i: (0, 0), pipeline_mode=pl.Buffered(1)),
        ],
        out_specs=pl.BlockSpec((tm, D), lambda i: (i, 0)),
        out_shape=jax.ShapeDtypeStruct((T, D), f32),
        compiler_params=pltpu.CompilerParams(
            dimension_semantics=("parallel",), vmem_limit_bytes=VMEM_LIMIT),
        name="outproj",
    )(x2d, ya, ob, g_b, w_bf)


def _mlp_kernel(x_ref, g_ref, w1_ref, w2_ref, gf_ref, o_ref, h_scr, *, final_norm, row_chunk):
    f = pl.program_id(1)
    last = pl.num_programs(1) - 1
    tm = o_ref.shape[0]
    chunks = [slice(r, r + row_chunk) for r in range(0, tm, row_chunk)]

    def ff(h):
        u = jnp.maximum(jnp.dot(h, w1_ref[...], preferred_element_type=f32), 0.0)
        return jnp.dot((u * u).astype(bf16), w2_ref[...], preferred_element_type=f32)

    @pl.when(f == 0)
    def _():
        for rows in chunks:
            x = x_ref[rows, :]
            ms = jnp.mean(x * x, axis=-1, keepdims=True)
            h = ((x * lax.rsqrt(ms + EPS)) * g_ref[...]).astype(bf16)
            h_scr[rows, :] = h
            o_ref[rows, :] = x + ff(h)

    @pl.when(jnp.logical_and(f > 0, jnp.logical_or(f < last, not final_norm)))
    def _():
        o_ref[...] += ff(h_scr[...])

    if final_norm:
        @pl.when(jnp.logical_and(f > 0, f == last))
        def _():
            for rows in chunks:
                y = o_ref[rows, :] + ff(h_scr[rows, :])
                ms = jnp.mean(y * y, axis=-1, keepdims=True)
                o_ref[rows, :] = (y * lax.rsqrt(ms + EPS)) * gf_ref[...]


def _mlp(x2d, g, w1_bf, w2_bf, gf, *, final_norm, tm=1024, tf=1024, row_chunk=256):
    T, D = x2d.shape
    d_ff = w1_bf.shape[1]
    tm = min(tm, T)
    assert T % tm == 0 and d_ff % tf == 0 and d_ff // tf >= 2 and tm % row_chunk == 0
    return pl.pallas_call(
        functools.partial(_mlp_kernel, final_norm=final_norm, row_chunk=row_chunk),
        grid=(T // tm, d_ff // tf),
        in_specs=[
            pl.BlockSpec((tm, D), lambda i, f: (i, 0)),
            pl.BlockSpec((1, D), lambda i, f: (0, 0)),
            pl.BlockSpec((D, tf), lambda i, f: (0, f)),
            pl.BlockSpec((tf, D), lambda i, f: (f, 0)),
            pl.BlockSpec((1, D), lambda i, f: (0, 0)),
        ],
        out_specs=pl.BlockSpec((tm, D), lambda i, f: (i, 0)),
        out_shape=jax.ShapeDtypeStruct((T, D), f32),
        scratch_shapes=[pltpu.VMEM((tm, D), bf16)],
        compiler_params=pltpu.CompilerParams(
            dimension_semantics=("parallel", "arbitrary"), vmem_limit_bytes=VMEM_LIMIT),
        name="mlp",
    )(x2d, g, w1_bf, w2_bf, gf)


def kernel(x, g_attn, w_in, b_in, sinks_a, g_out_a, g_out_b, w_out, g_mlp, w_1, w_2, g_final):
    b, S, D = x.shape
    depth = w_in.shape[0]
    T = b * S
    bias_a = jnp.asarray(_band_bias(WINDOW_A - 1, 1))
    bias_b = jnp.asarray(np.stack([_band_bias(w // d, d) for w, d in DILATED_BRANCHES], axis=1))

    x2d = x.reshape(T, D)
    for l in range(depth):
        qa, kva, *qkv_b = _inproj(x2d, g_attn[l][None], w_in[l].astype(bf16), b_in[l][None], b)
        r3 = lambda t: t.reshape(b, S, t.shape[-1])
        ya = _attention_a(r3(qa), r3(kva), bias_a, sinks_a[l], g_out_a[l][None])
        ob = _attention_b(qkv_b, bias_b, b)
        x1 = _outproj(x2d, ya.reshape(T, Q_W), ob.reshape(T, Q_W), g_out_b[l][None],
                      w_out[l].astype(bf16))
        x2d = _mlp(x1, g_mlp[l][None], w_1[l].astype(bf16), w_2[l].astype(bf16), g_final[None],
                   final_norm=(l == depth - 1))
    return x2d.reshape(b, S, D)
```

```python
import functools

import numpy as np
import jax
import jax.numpy as jnp
from jax import lax
from jax.experimental import pallas as pl
from jax.experimental.pallas import tpu as pltpu

HEAD_DIM = 64
N_HEADS = 16
N_KV_A = 2
BLOCK = 128
WINDOW_A = 128
DILATED_BRANCHES = ((128, 1), (512, 4), (2048, 16))
EPS = 1e-5
NEG_INF = -1e30
LOG2E = 1.4426950408889634
Q_W = N_HEADS * HEAD_DIM
KV_A_W = N_KV_A * HEAD_DIM
LANES = 128
PACK = 16
VMEM_LIMIT = 60 * 1024 * 1024

bf16 = jnp.bfloat16
f32 = jnp.float32
_NT = (((1,), (1,)), ((), ()))


def _alibi_slopes(n):
    return 2.0 ** (-8.0 * (np.arange(n) + 1) / n)


def _band_bias(max_steps, step_dist):
    qi = np.arange(BLOCK)[:, None]
    kj = np.arange(2 * BLOCK)[None, :]
    steps = qi + BLOCK - kj
    valid = (steps >= 0) & (steps <= max_steps)
    slopes = _alibi_slopes(N_HEADS).astype(np.float32)
    pen = slopes[:, None, None] * (step_dist * steps).astype(np.float32)[None]
    pen = pen.astype(np.float64) * LOG2E
    gen = np.where(valid[None], -pen, np.float32(NEG_INF)).astype(np.float32)
    first = np.where((valid & (kj >= BLOCK))[None], -pen, np.float32(NEG_INF)).astype(np.float32)
    return np.stack([gen, first])


def _inproj_kernel(x_ref, g_ref, w_ref, b_ref, qa_ref, kva_ref, b1_ref, b4_ref, b16_ref,
                   ys, ys4, *, chunk, row_chunk):
    tm = x_ref.shape[0]
    qscale = HEAD_DIM ** -0.5 * LOG2E
    n4, n16 = row_chunk // 4, row_chunk // 16
    c0 = Q_W + 2 * KV_A_W
    slot = 0
    for part in range(tm // row_chunk):
        rows = slice(part * row_chunk, (part + 1) * row_chunk)
        x = x_ref[rows, :]
        ms = jnp.mean(x * x, axis=-1, keepdims=True)
        h = ((x * lax.rsqrt(ms + EPS)) * g_ref[...]).astype(bf16)

        def project(col, ch, scale):
            y = jnp.dot(h, w_ref[:, col:col + ch], preferred_element_type=f32)
            y = y + b_ref[:, col:col + ch]
            return y if scale is None else y * scale

        for cc in range(0, Q_W, chunk):
            qa_ref[rows, cc:cc + chunk] = project(cc, chunk, qscale).astype(bf16)
        kva_ref[rows, :] = project(Q_W, 2 * KV_A_W, None).astype(bf16)

        rows4 = slice(part * n4, (part + 1) * n4)
        rows16 = slice(part * n16, (part + 1) * n16)
        for cc in range(0, 3 * Q_W, chunk):
            y = project(c0 + cc, chunk, qscale if cc < Q_W else None)
            b1_ref[rows, cc:cc + chunk] = y.astype(bf16)
            for s in range(chunk // LANES):
                lanes = slice(cc + s * LANES, cc + (s + 1) * LANES)
                ys[slot, s] = y[:, s * LANES:(s + 1) * LANES]
                for r in range(4):
                    t = ys[slot, s, pl.ds(r, n4, stride=4), :]
                    b4_ref[0, r, rows4, lanes] = t.astype(bf16)
                    ys4[slot, s, r * n4:(r + 1) * n4, :] = t
                for r in range(4):
                    for c in range(4):
                        b16_ref[0, r + 4 * c, rows16, lanes] = ys4[
                            slot, s, pl.ds(r * n4 + c, n16, stride=4), :].astype(bf16)
            slot = 1 - slot


def _inproj(x2d, g, w_bf, b_in, batch, *, tm=512, chunk=512, row_chunk=512):
    T, D = x2d.shape
    S = T // batch
    d_in = w_bf.shape[1]
    assert 2 * KV_A_W + 4 * Q_W == d_in and S % tm == 0 and Q_W % chunk == 0
    tiles = S // tm
    const = lambda i: (0, 0)
    row = lambda i: (i, 0)
    res = lambda i: (i // tiles, 0, i % tiles, 0)
    return pl.pallas_call(
        functools.partial(_inproj_kernel, chunk=chunk, row_chunk=row_chunk),
        grid=(T // tm,),
        in_specs=[
            pl.BlockSpec((tm, D), row),
            pl.BlockSpec((1, D), const),
            pl.BlockSpec((D, d_in), const, pipeline_mode=pl.Buffered(1)),
            pl.BlockSpec((1, d_in), const),
        ],
        out_specs=[pl.BlockSpec((tm, Q_W), row),
                   pl.BlockSpec((tm, 2 * KV_A_W), row),
                   pl.BlockSpec((tm, 3 * Q_W), row),
                   pl.BlockSpec((1, 4, tm // 4, 3 * Q_W), res),
                   pl.BlockSpec((1, 16, tm // 16, 3 * Q_W), res)],
        out_shape=[jax.ShapeDtypeStruct((T, Q_W), bf16),
                   jax.ShapeDtypeStruct((T, 2 * KV_A_W), bf16),
                   jax.ShapeDtypeStruct((T, 3 * Q_W), bf16),
                   jax.ShapeDtypeStruct((batch, 4, S // 4, 3 * Q_W), bf16),
                   jax.ShapeDtypeStruct((batch, 16, S // 16, 3 * Q_W), bf16)],
        scratch_shapes=[pltpu.VMEM((2, chunk // LANES, row_chunk, LANES), f32),
                        pltpu.VMEM((2, chunk // LANES, row_chunk, LANES), f32)],
        compiler_params=pltpu.CompilerParams(
            dimension_semantics=("parallel",), vmem_limit_bytes=VMEM_LIMIT),
        name="inproj",
    )(x2d, g, w_bf, b_in)


def _attn_a_kernel(q_ref, k_ref, kp_ref, v_ref, vp_ref, bias_ref, sink_ref, g_ref, out_ref,
                   k_scr, v_scr, o_scr, *, tq):
    li = pl.program_id(1)
    lane = lax.broadcasted_iota(jnp.int32, (1, LANES), 1)
    lo = lane < HEAD_DIM

    def dup(t):
        r = pltpu.roll(t.astype(f32), HEAD_DIM, 1).astype(bf16)
        return jnp.concatenate([jnp.where(lo, t, r), jnp.where(lo, r, t)], axis=1)
    k_scr[0:BLOCK, :] = dup(kp_ref[0])
    k_scr[BLOCK:, :] = dup(k_ref[0])
    v_scr[0:BLOCK, :] = dup(vp_ref[0])
    v_scr[BLOCK:, :] = dup(v_ref[0])

    first = jnp.where(li == 0, 1, 0)
    ones = jnp.ones((2 * BLOCK, LANES), bf16)
    lane0 = lane == 0
    row0 = lax.broadcasted_iota(jnp.int32, (PACK, 1), 0) == 0
    assert WINDOW_A <= BLOCK
    for j in range(tq // BLOCK):
        rows = slice(j * BLOCK, (j + 1) * BLOCK)
        krows = slice(j * BLOCK, (j + 2) * BLOCK)
        sel = first if j == 0 else 0
        for hp in range(N_HEADS // 2):
            cols = slice(hp * LANES, (hp + 1) * LANES)
            g = (2 * hp) // (N_HEADS // N_KV_A)
            kcols = slice(g * LANES, (g + 1) * LANES)
            q2 = q_ref[0, rows, cols]
            k2 = k_scr[krows, kcols]
            v_top = jnp.where(row0, jnp.zeros((PACK, LANES), bf16),
                              v_scr[j * BLOCK:j * BLOCK + PACK, kcols])
            v2 = jnp.concatenate([v_top, v_scr[j * BLOCK + PACK:(j + 2) * BLOCK, kcols]], axis=0)
            v2a = jnp.concatenate([v2, ones], axis=1)
            zero = jnp.zeros_like(q2)
            pv, l = [], []
            for e in range(2):
                h = 2 * hp + e
                qm = jnp.where(lo, q2, zero) if e == 0 else jnp.where(lo, zero, q2)
                s = lax.dot_general(qm, k2, _NT, preferred_element_type=f32)
                s_prev = jnp.where(lane0, sink_ref[h] * LOG2E,
                                   s[:, :BLOCK] + bias_ref[sel, h, :, :BLOCK])
                s = jnp.concatenate([s_prev, s[:, BLOCK:] + bias_ref[sel, h, :, BLOCK:]], axis=1)
                p = jnp.exp2(s - jnp.max(s, axis=-1, keepdims=True))
                pva = jnp.dot(p.astype(bf16), v2a, preferred_element_type=f32)
                pv.append(pva[:, :LANES])
                l.append(pva[:, LANES:])
            o_scr[rows, cols] = jnp.where(lo, pv[0], pv[1]) / jnp.where(lo, l[0], l[1])

    o = o_scr[...]
    ms = jnp.mean(o * o, axis=-1, keepdims=True)
    out_ref[0] = ((o * lax.rsqrt(ms + EPS)) * g_ref[...]).astype(bf16)


def _attention_a(q, kv, bias, sinks, gain, *, tq=1024):
    b, S, _ = q.shape
    tq = min(tq, S)
    assert S % tq == 0 and tq % BLOCK == 0
    nsub = tq // BLOCK
    cur = lambda bi, li: (bi, li, 0)
    k, v = kv, kv
    return pl.pallas_call(
        functools.partial(_attn_a_kernel, tq=tq),
        grid=(b, S // tq),
        in_specs=[
            pl.BlockSpec((1, tq, Q_W), cur),
            pl.BlockSpec((1, tq, KV_A_W), lambda bi, li: (bi, li, 0)),
            pl.BlockSpec((1, BLOCK, KV_A_W), lambda bi, li: (bi, jnp.maximum(li * nsub - 1, 0), 0)),
            pl.BlockSpec((1, tq, KV_A_W), lambda bi, li: (bi, li, 1)),
            pl.BlockSpec((1, BLOCK, KV_A_W), lambda bi, li: (bi, jnp.maximum(li * nsub - 1, 0), 1)),
            pl.BlockSpec(bias.shape, lambda bi, li: (0, 0, 0, 0)),
            pl.BlockSpec(memory_space=pltpu.SMEM),
            pl.BlockSpec((1, Q_W), lambda bi, li: (0, 0)),
        ],
        out_specs=pl.BlockSpec((1, tq, Q_W), cur),
        out_shape=jax.ShapeDtypeStruct((b, S, Q_W), bf16),
        scratch_shapes=[pltpu.VMEM((tq + BLOCK, 2 * KV_A_W), bf16),
                        pltpu.VMEM((tq + BLOCK, 2 * KV_A_W), bf16),
                        pltpu.VMEM((tq, Q_W), f32)],
        compiler_params=pltpu.CompilerParams(
            dimension_semantics=("parallel", "arbitrary"), vmem_limit_bytes=VMEM_LIMIT),
        name="attn_a",
    )(q, k, k, v, v, bias, sinks, gain)


B_TILE = 2048
B_DILS = tuple(d for _, d in DILATED_BRANCHES)
assert B_DILS == (1, 4, 16) and all(w // d == BLOCK for w, d in DILATED_BRANCHES)
B_NSUB = B_TILE // BLOCK
B_PAIRS = 2


def _attn_b_kernel(q1, q4, q16, k1, k1h, k4, k4h, k16, k16h, v1, v1h, v4, v4h, v16, v16h,
                   bias_ref, o_ref, pv_st, l_st, m_st):
    first = jnp.where(pl.program_id(2) == 0, 1, 0)
    lane = lax.broadcasted_iota(jnp.int32, (1, LANES), 1)
    lo = lane < HEAD_DIM
    ones = jnp.ones((2 * BLOCK, LANES), bf16)

    def pair_block(br, h0, q2, k2, v2, sel):
        v2a = jnp.concatenate([v2, ones], axis=1)
        zero = jnp.zeros_like(q2)
        pv, l, m = [], [], []
        for e in range(2):
            qm = jnp.where(lo, q2, zero) if e == 0 else jnp.where(lo, zero, q2)
            s = lax.dot_general(qm, k2, _NT, preferred_element_type=f32)
            s = s + bias_ref[sel, br, h0 + e]
            me = jnp.max(s, axis=-1, keepdims=True)
            p = jnp.exp2(s - me)
            pva = jnp.dot(p.astype(bf16), v2a, preferred_element_type=f32)
            pv.append(pva[:, :LANES])
            l.append(pva[:, LANES:])
            m.append(jnp.broadcast_to(me, (BLOCK, LANES)))
        return (jnp.where(lo, pv[0], pv[1]), jnp.where(lo, l[0], l[1]),
                jnp.where(lo, m[0], m[1]))

    def window(cur, halo, j, lanes):
        if j == 0:
            return jnp.concatenate([halo[:, lanes], cur[0:BLOCK, lanes]], axis=0)
        return cur[(j - 1) * BLOCK:(j + 1) * BLOCK, lanes]

    n4 = B_TILE // 4
    for pair in range(B_PAIRS):
        lanes = slice(pair * LANES, (pair + 1) * LANES)

        def keep(slot, rows, res):
            s = slot * B_PAIRS + pair
            pv_st[s, rows, :], l_st[s, rows, :], m_st[s, rows, :] = res

        def kept(slot, rows):
            s = slot * B_PAIRS + pair
            return pv_st[s, rows, :], l_st[s, rows, :], m_st[s, rows, :]

        def block(br, q2, k_cur, k_halo, v_cur, v_halo, j):
            return pair_block(br, 2 * pair, q2, window(k_cur, k_halo, j, lanes),
                              window(v_cur, v_halo, j, lanes), first if j == 0 else 0)

        for r in range(4):
            for c in range(4):
                r16 = r + 4 * c
                keep(2, pl.ds(r * n4 + c, BLOCK, stride=4),
                     block(2, q16[0, r16, :, lanes], k16.at[0, r16], k16h.at[0, r16],
                           v16.at[0, r16], v16h.at[0, r16], 0))
            for j in range(n4 // BLOCK):
                keep(1, pl.ds(j * 4 * BLOCK + r, BLOCK, stride=4),
                     kept(2, slice(r * n4 + j * BLOCK, r * n4 + (j + 1) * BLOCK)))
        for r in range(4):
            for j in range(B_NSUB // 4):
                keep(0, pl.ds(j * 4 * BLOCK + r, BLOCK, stride=4),
                     block(1, q4[0, r, j * BLOCK:(j + 1) * BLOCK, lanes], k4.at[0, r],
                           k4h.at[0, r], v4.at[0, r], v4h.at[0, r], j))
        for j in range(B_NSUB):
            rows = slice(j * BLOCK, (j + 1) * BLOCK)
            parts = [block(0, q1[0, rows, lanes], k1.at[0], k1h.at[0], v1.at[0], v1h.at[0], j),
                     kept(0, rows), kept(1, rows)]
            mx = jnp.maximum(jnp.maximum(parts[0][2], parts[1][2]), parts[2][2])
            num = den = None
            for pv_n, l_n, m_n in parts:
                w = jnp.exp2(m_n - mx)
                num = w * pv_n if num is None else num + w * pv_n
                den = w * l_n if den is None else den + w * l_n
            o_ref[0, rows, lanes] = num / den


def _attention_b(qkv, bias, batch):
    b1, b4, b16 = qkv
    S = b1.shape[0] // batch
    assert S % B_TILE == 0
    b1 = b1.reshape(batch, S, 3 * Q_W)
    nstep = N_HEADS // (2 * B_PAIRS)
    W = B_PAIRS * LANES
    n4, n16 = B_TILE // 4, B_TILE // 16
    assert n16 == BLOCK

    def specs(part):
        col = lambda hg: part * nstep + hg
        nat = pl.BlockSpec((1, B_TILE, W), lambda bi, hg, ti: (bi, ti, col(hg)))
        nat_h = pl.BlockSpec((1, BLOCK, W), lambda bi, hg, ti: (
            bi, jnp.maximum(ti * B_NSUB - 1, 0), col(hg)))
        d4 = pl.BlockSpec((1, 4, n4, W), lambda bi, hg, ti: (bi, 0, ti, col(hg)))
        d4_h = pl.BlockSpec((1, 4, BLOCK, W), lambda bi, hg, ti: (
            bi, 0, jnp.maximum(ti * (n4 // BLOCK) - 1, 0), col(hg)))
        d16 = pl.BlockSpec((1, 16, n16, W), lambda bi, hg, ti: (bi, 0, ti, col(hg)))
        d16_h = pl.BlockSpec((1, 16, BLOCK, W), lambda bi, hg, ti: (
            bi, 0, jnp.maximum(ti - 1, 0), col(hg)))
        return nat, nat_h, d4, d4_h, d16, d16_h

    q_nat, _, q_d4, _, q_d16, _ = specs(0)
    kv_args = (b1, b1, b4, b4, b16, b16)
    state = pltpu.VMEM((3 * B_PAIRS, B_TILE, LANES), f32)
    return pl.pallas_call(
        _attn_b_kernel,
        grid=(batch, nstep, S // B_TILE),
        in_specs=[q_nat, q_d4, q_d16, *specs(1), *specs(2),
                  pl.BlockSpec((2, 3, 2 * B_PAIRS, BLOCK, 2 * BLOCK),
                               lambda bi, hg, ti: (0, 0, hg, 0, 0))],
        out_specs=pl.BlockSpec((1, B_TILE, W), lambda bi, hg, ti: (bi, ti, hg)),
        out_shape=jax.ShapeDtypeStruct((batch, S, Q_W), f32),
        scratch_shapes=[state, state, state],
        compiler_params=pltpu.CompilerParams(
            dimension_semantics=("parallel", "parallel", "arbitrary"),
            vmem_limit_bytes=VMEM_LIMIT),
        name="attn_b",
    )(b1, b4, b16, *kv_args, *kv_args, bias)


def _outproj_kernel(x_ref, ya_ref, ob_ref, g_ref, w_ref, o_ref):
    ob = ob_ref[...]
    ms = jnp.mean(ob * ob, axis=-1, keepdims=True)
    yb = ((ob * lax.rsqrt(ms + EPS)) * g_ref[...]).astype(bf16)
    acc = jnp.dot(ya_ref[...], w_ref[0:Q_W, :], preferred_element_type=f32)
    acc = acc + jnp.dot(yb, w_ref[Q_W:, :], preferred_element_type=f32)
    o_ref[...] = x_ref[...] + acc


def _outproj(x2d, ya, ob, g_b, w_bf, *, tm=512):
    T, D = x2d.shape
    return pl.pallas_call(
        _outproj_kernel,
        grid=(T // tm,),
        in_specs=[
            pl.BlockSpec((tm, D), lambda i: (i, 0)),
            pl.BlockSpec((tm, Q_W), lambda i: (i, 0)),
            pl.BlockSpec((tm, Q_W), lambda i: (i, 0)),
            pl.BlockSpec((1, Q_W), lambda i: (0, 0)),
            pl.BlockSpec(w_bf.shape, lambda i: (0, 0), pipeline_mode=pl.Buffered(1)),
        ],
        out_specs=pl.BlockSpec((tm, D), lambda i: (i, 0)),
        out_shape=jax.ShapeDtypeStruct((T, D), f32),
        compiler_params=pltpu.CompilerParams(
            dimension_semantics=("parallel",), vmem_limit_bytes=VMEM_LIMIT),
        name="outproj",
    )(x2d, ya, ob, g_b, w_bf)


def _mlp_kernel(x_ref, g_ref, w1_ref, w2_ref, gf_ref, o_ref, h_scr, *, final_norm, row_chunk):
    f = pl.program_id(1)
    last = pl.num_programs(1) - 1
    tm = o_ref.shape[0]
    chunks = [slice(r, r + row_chunk) for r in range(0, tm, row_chunk)]

    def ff(h):
        u = jnp.maximum(jnp.dot(h, w1_ref[...], preferred_element_type=f32), 0.0)
        return jnp.dot((u * u).astype(bf16), w2_ref[...], preferred_element_type=f32)

    @pl.when(f == 0)
    def _():
        for rows in chunks:
            x = x_ref[rows, :]
            ms = jnp.mean(x * x, axis=-1, keepdims=True)
            h = ((x * lax.rsqrt(ms + EPS)) * g_ref[...]).astype(bf16)
            h_scr[rows, :] = h
            o_ref[rows, :] = x + ff(h)

    @pl.when(jnp.logical_and(f > 0, jnp.logical_or(f < last, not final_norm)))
    def _():
        o_ref[...] += ff(h_scr[...])

    if final_norm:
        @pl.when(jnp.logical_and(f > 0, f == last))
        def _():
            for rows in chunks:
                y = o_ref[rows, :] + ff(h_scr[rows, :])
                ms = jnp.mean(y * y, axis=-1, keepdims=True)
                o_ref[rows, :] = (y * lax.rsqrt(ms + EPS)) * gf_ref[...]


def _mlp(x2d, g, w1_bf, w2_bf, gf, *, final_norm, tm=1024, tf=1024, row_chunk=256):
    T, D = x2d.shape
    d_ff = w1_bf.shape[1]
    tm = min(tm, T)
    assert T % tm == 0 and d_ff % tf == 0 and d_ff // tf >= 2 and tm % row_chunk == 0
    return pl.pallas_call(
        functools.partial(_mlp_kernel, final_norm=final_norm, row_chunk=row_chunk),
        grid=(T // tm, d_ff // tf),
        in_specs=[
            pl.BlockSpec((tm, D), lambda i, f: (i, 0)),
            pl.BlockSpec((1, D), lambda i, f: (0, 0)),
            pl.BlockSpec((D, tf), lambda i, f: (0, f)),
            pl.BlockSpec((tf, D), lambda i, f: (f, 0)),
            pl.BlockSpec((1, D), lambda i, f: (0, 0)),
        ],
        out_specs=pl.BlockSpec((tm, D), lambda i, f: (i, 0)),
        out_shape=jax.ShapeDtypeStruct((T, D), f32),
        scratch_shapes=[pltpu.VMEM((tm, D), bf16)],
        compiler_params=pltpu.CompilerParams(
            dimension_semantics=("parallel", "arbitrary"), vmem_limit_bytes=VMEM_LIMIT),
        name="mlp",
    )(x2d, g, w1_bf, w2_bf, gf)


def kernel(x, g_attn, w_in, b_in, sinks_a, g_out_a, g_out_b, w_out, g_mlp, w_1, w_2, g_final):
    b, S, D = x.shape
    depth = w_in.shape[0]
    T = b * S
    bias_a = jnp.asarray(_band_bias(WINDOW_A - 1, 1))
    bias_b = jnp.asarray(np.stack([_band_bias(w // d, d) for w, d in DILATED_BRANCHES], axis=1))

    x2d = x.reshape(T, D)
    for l in range(depth):
        qa, kva, *qkv_b = _inproj(x2d, g_attn[l][None], w_in[l].astype(bf16), b_in[l][None], b)
        r3 = lambda t: t.reshape(b, S, t.shape[-1])
        ya = _attention_a(r3(qa), r3(kva), bias_a, sinks_a[l], g_out_a[l][None])
        ob = _attention_b(qkv_b, bias_b, b)
        x1 = _outproj(x2d, ya.reshape(T, Q_W), ob.reshape(T, Q_W), g_out_b[l][None],
                      w_out[l].astype(bf16))
        x2d = _mlp(x1, g_mlp[l][None], w_1[l].astype(bf16), w_2[l].astype(bf16), g_final[None],
                   final_norm=(l == depth - 1))
    return x2d.reshape(b, S, D)
```

```python
import functools

import numpy as np
import jax
import jax.numpy as jnp
from jax import lax
from jax.experimental import pallas as pl
from jax.experimental.pallas import tpu as pltpu

HEAD_DIM = 64
N_HEADS = 16
N_KV_A = 2
BLOCK = 128
WINDOW_A = 128
DILATED_BRANCHES = ((128, 1), (512, 4), (2048, 16))
EPS = 1e-5
NEG_INF = -1e30
LOG2E = 1.4426950408889634
Q_W = N_HEADS * HEAD_DIM
KV_A_W = N_KV_A * HEAD_DIM
LANES = 128
PACK = 16
VMEM_LIMIT = 60 * 1024 * 1024

bf16 = jnp.bfloat16
f32 = jnp.float32
_NT = (((1,), (1,)), ((), ()))


def _alibi_slopes(n):
    return 2.0 ** (-8.0 * (np.arange(n) + 1) / n)


def _band_bias(max_steps, step_dist):
    qi = np.arange(BLOCK)[:, None]
    kj = np.arange(2 * BLOCK)[None, :]
    steps = qi + BLOCK - kj
    valid = (steps >= 0) & (steps <= max_steps)
    slopes = _alibi_slopes(N_HEADS).astype(np.float32)
    pen = slopes[:, None, None] * (step_dist * steps).astype(np.float32)[None]
    pen = pen.astype(np.float64) * LOG2E
    gen = np.where(valid[None], -pen, np.float32(NEG_INF)).astype(np.float32)
    first = np.where((valid & (kj >= BLOCK))[None], -pen, np.float32(NEG_INF)).astype(np.float32)
    return np.stack([gen, first])


def _inproj_kernel(x_ref, g_ref, w_ref, b_ref, qa_ref, kva_ref, b1_ref, b4_ref, b16_ref,
                   ys, ys4, *, chunk, row_chunk):
    tm = x_ref.shape[0]
    qscale = HEAD_DIM ** -0.5 * LOG2E
    n4, n16 = row_chunk // 4, row_chunk // 16
    c0 = Q_W + 2 * KV_A_W
    slot = 0
    for part in range(tm // row_chunk):
        rows = slice(part * row_chunk, (part + 1) * row_chunk)
        x = x_ref[rows, :]
        ms = jnp.mean(x * x, axis=-1, keepdims=True)
        h = ((x * lax.rsqrt(ms + EPS)) * g_ref[...]).astype(bf16)

        def project(col, ch, scale):
            y = jnp.dot(h, w_ref[:, col:col + ch], preferred_element_type=f32)
            y = y + b_ref[:, col:col + ch]
            return y if scale is None else y * scale

        for cc in range(0, Q_W, chunk):
            qa_ref[rows, cc:cc + chunk] = project(cc, chunk, qscale).astype(bf16)
        kva_ref[rows, :] = project(Q_W, 2 * KV_A_W, None).astype(bf16)

        rows4 = slice(part * n4, (part + 1) * n4)
        rows16 = slice(part * n16, (part + 1) * n16)
        for cc in range(0, 3 * Q_W, chunk):
            y = project(c0 + cc, chunk, qscale if cc < Q_W else None)
            b1_ref[rows, cc:cc + chunk] = y.astype(bf16)
            for s in range(chunk // LANES):
                lanes = slice(cc + s * LANES, cc + (s + 1) * LANES)
                ys[slot, s] = y[:, s * LANES:(s + 1) * LANES]
                for r in range(4):
                    t = ys[slot, s, pl.ds(r, n4, stride=4), :]
                    b4_ref[0, r, rows4, lanes] = t.astype(bf16)
                    ys4[slot, s, r * n4:(r + 1) * n4, :] = t
                for r in range(4):
                    for c in range(4):
                        b16_ref[0, r + 4 * c, rows16, lanes] = ys4[
                            slot, s, pl.ds(r * n4 + c, n16, stride=4), :].astype(bf16)
            slot = 1 - slot


def _inproj(x2d, g, w_bf, b_in, batch, *, tm=512, chunk=512, row_chunk=512):
    T, D = x2d.shape
    S = T // batch
    d_in = w_bf.shape[1]
    assert 2 * KV_A_W + 4 * Q_W == d_in and S % tm == 0 and Q_W % chunk == 0
    tiles = S // tm
    const = lambda i: (0, 0)
    row = lambda i: (i, 0)
    res = lambda i: (i // tiles, 0, i % tiles, 0)
    return pl.pallas_call(
        functools.partial(_inproj_kernel, chunk=chunk, row_chunk=row_chunk),
        grid=(T // tm,),
        in_specs=[
            pl.BlockSpec((tm, D), row),
            pl.BlockSpec((1, D), const),
            pl.BlockSpec((D, d_in), const, pipeline_mode=pl.Buffered(1)),
            pl.BlockSpec((1, d_in), const),
        ],
        out_specs=[pl.BlockSpec((tm, Q_W), row),
                   pl.BlockSpec((tm, 2 * KV_A_W), row),
                   pl.BlockSpec((tm, 3 * Q_W), row),
                   pl.BlockSpec((1, 4, tm // 4, 3 * Q_W), res),
                   pl.BlockSpec((1, 16, tm // 16, 3 * Q_W), res)],
        out_shape=[jax.ShapeDtypeStruct((T, Q_W), bf16),
                   jax.ShapeDtypeStruct((T, 2 * KV_A_W), bf16),
                   jax.ShapeDtypeStruct((T, 3 * Q_W), bf16),
                   jax.ShapeDtypeStruct((batch, 4, S // 4, 3 * Q_W), bf16),
                   jax.ShapeDtypeStruct((batch, 16, S // 16, 3 * Q_W), bf16)],
        scratch_shapes=[pltpu.VMEM((2, chunk // LANES, row_chunk, LANES), f32),
                        pltpu.VMEM((2, chunk // LANES, row_chunk, LANES), f32)],
        compiler_params=pltpu.CompilerParams(
            dimension_semantics=("parallel",), vmem_limit_bytes=VMEM_LIMIT),
        name="inproj",
    )(x2d, g, w_bf, b_in)


def _attn_a_kernel(q_ref, k_ref, kp_ref, v_ref, vp_ref, bias_ref, sink_ref, g_ref, out_ref,
                   k_scr, v_scr, o_scr, *, tq):
    li = pl.program_id(1)
    lane = lax.broadcasted_iota(jnp.int32, (1, LANES), 1)
    lo = lane < HEAD_DIM

    def dup(t):
        r = pltpu.roll(t.astype(f32), HEAD_DIM, 1).astype(bf16)
        return jnp.concatenate([jnp.where(lo, t, r), jnp.where(lo, r, t)], axis=1)
    k_scr[0:BLOCK, :] = dup(kp_ref[0])
    k_scr[BLOCK:, :] = dup(k_ref[0])
    v_scr[0:BLOCK, :] = dup(vp_ref[0])
    v_scr[BLOCK:, :] = dup(v_ref[0])

    first = jnp.where(li == 0, 1, 0)
    ones = jnp.ones((2 * BLOCK, LANES), bf16)
    lane0 = lane == 0
    row0 = lax.broadcasted_iota(jnp.int32, (PACK, 1), 0) == 0
    assert WINDOW_A <= BLOCK
    for j in range(tq // BLOCK):
        rows = slice(j * BLOCK, (j + 1) * BLOCK)
        krows = slice(j * BLOCK, (j + 2) * BLOCK)
        sel = first if j == 0 else 0
        for hp in range(N_HEADS // 2):
            cols = slice(hp * LANES, (hp + 1) * LANES)
            g = (2 * hp) // (N_HEADS // N_KV_A)
            kcols = slice(g * LANES, (g + 1) * LANES)
            q2 = q_ref[0, rows, cols]
            k2 = k_scr[krows, kcols]
            v_top = jnp.where(row0, jnp.zeros((PACK, LANES), bf16),
                              v_scr[j * BLOCK:j * BLOCK + PACK, kcols])
            v2 = jnp.concatenate([v_top, v_scr[j * BLOCK + PACK:(j + 2) * BLOCK, kcols]], axis=0)
            v2a = jnp.concatenate([v2, ones], axis=1)
            zero = jnp.zeros_like(q2)
            pv, l = [], []
            for e in range(2):
                h = 2 * hp + e
                qm = jnp.where(lo, q2, zero) if e == 0 else jnp.where(lo, zero, q2)
                s = lax.dot_general(qm, k2, _NT, preferred_element_type=f32)
                s_prev = jnp.where(lane0, sink_ref[h] * LOG2E,
                                   s[:, :BLOCK] + bias_ref[sel, h, :, :BLOCK])
                s = jnp.concatenate([s_prev, s[:, BLOCK:] + bias_ref[sel, h, :, BLOCK:]], axis=1)
                p = jnp.exp2(s - jnp.max(s, axis=-1, keepdims=True))
                pva = jnp.dot(p.astype(bf16), v2a, preferred_element_type=f32)
                pv.append(pva[:, :LANES])
                l.append(pva[:, LANES:])
            o_scr[rows, cols] = jnp.where(lo, pv[0], pv[1]) / jnp.where(lo, l[0], l[1])

    o = o_scr[...]
    ms = jnp.mean(o * o, axis=-1, keepdims=True)
    out_ref[0] = ((o * lax.rsqrt(ms + EPS)) * g_ref[...]).astype(bf16)


def _attention_a(q, kv, bias, sinks, gain, *, tq=1024):
    b, S, _ = q.shape
    tq = min(tq, S)
    assert S % tq == 0 and tq % BLOCK == 0
    nsub = tq // BLOCK
    cur = lambda bi, li: (bi, li, 0)
    k, v = kv, kv
    return pl.pallas_call(
        functools.partial(_attn_a_kernel, tq=tq),
        grid=(b, S // tq),
        in_specs=[
            pl.BlockSpec((1, tq, Q_W), cur),
            pl.BlockSpec((1, tq, KV_A_W), lambda bi, li: (bi, li, 0)),
            pl.BlockSpec((1, BLOCK, KV_A_W), lambda bi, li: (bi, jnp.maximum(li * nsub - 1, 0), 0)),
            pl.BlockSpec((1, tq, KV_A_W), lambda bi, li: (bi, li, 1)),
            pl.BlockSpec((1, BLOCK, KV_A_W), lambda bi, li: (bi, jnp.maximum(li * nsub - 1, 0), 1)),
            pl.BlockSpec(bias.shape, lambda bi, li: (0, 0, 0, 0)),
            pl.BlockSpec(memory_space=pltpu.SMEM),
            pl.BlockSpec((1, Q_W), lambda bi, li: (0, 0)),
        ],
        out_specs=pl.BlockSpec((1, tq, Q_W), cur),
        out_shape=jax.ShapeDtypeStruct((b, S, Q_W), bf16),
        scratch_shapes=[pltpu.VMEM((tq + BLOCK, 2 * KV_A_W), bf16),
                        pltpu.VMEM((tq + BLOCK, 2 * KV_A_W), bf16),
                        pltpu.VMEM((tq, Q_W), f32)],
        compiler_params=pltpu.CompilerParams(
            dimension_semantics=("parallel", "arbitrary"), vmem_limit_bytes=VMEM_LIMIT),
        name="attn_a",
    )(q, k, k, v, v, bias, sinks, gain)


B_TILE = 2048
B_DILS = tuple(d for _, d in DILATED_BRANCHES)
assert B_DILS == (1, 4, 16) and all(w // d == BLOCK for w, d in DILATED_BRANCHES)
B_NSUB = B_TILE // BLOCK
B_PAIRS = 2


def _attn_b_kernel(q1, q4, q16, k1, k1h, k4, k4h, k16, k16h, v1, v1h, v4, v4h, v16, v16h,
                   bias_ref, o_ref, pv_st, l_st, m_st):
    first = jnp.where(pl.program_id(2) == 0, 1, 0)
    lane = lax.broadcasted_iota(jnp.int32, (1, LANES), 1)
    lo = lane < HEAD_DIM
    ones = jnp.ones((2 * BLOCK, LANES), bf16)

    def pair_block(br, h0, q2, k2, v2, sel):
        v2a = jnp.concatenate([v2, ones], axis=1)
        zero = jnp.zeros_like(q2)
        pv, l, m = [], [], []
        for e in range(2):
            qm = jnp.where(lo, q2, zero) if e == 0 else jnp.where(lo, zero, q2)
            s = lax.dot_general(qm, k2, _NT, preferred_element_type=f32)
            s = s + bias_ref[sel, br, h0 + e]
            me = jnp.max(s, axis=-1, keepdims=True)
            p = jnp.exp2(s - me)
            pva = jnp.dot(p.astype(bf16), v2a, preferred_element_type=f32)
            pv.append(pva[:, :LANES])
            l.append(pva[:, LANES:])
            m.append(jnp.broadcast_to(me, (BLOCK, LANES)))
        return (jnp.where(lo, pv[0], pv[1]), jnp.where(lo, l[0], l[1]),
                jnp.where(lo, m[0], m[1]))

    def window(cur, halo, j, lanes):
        if j == 0:
            return jnp.concatenate([halo[:, lanes], cur[0:BLOCK, lanes]], axis=0)
        return cur[(j - 1) * BLOCK:(j + 1) * BLOCK, lanes]

    n4 = B_TILE // 4
    for pair in range(B_PAIRS):
        lanes = slice(pair * LANES, (pair + 1) * LANES)

        def keep(slot, rows, res):
            s = slot * B_PAIRS + pair
            pv_st[s, rows, :], l_st[s, rows, :], m_st[s, rows, :] = res

        def kept(slot, rows):
            s = slot * B_PAIRS + pair
            return pv_st[s, rows, :], l_st[s, rows, :], m_st[s, rows, :]

        def block(br, q2, k_cur, k_halo, v_cur, v_halo, j):
            return pair_block(br, 2 * pair, q2, window(k_cur, k_halo, j, lanes),
                              window(v_cur, v_halo, j, lanes), first if j == 0 else 0)

        for r in range(4):
            for c in range(4):
                r16 = r + 4 * c
                keep(2, pl.ds(r * n4 + c, BLOCK, stride=4),
                     block(2, q16[0, r16, :, lanes], k16.at[0, r16], k16h.at[0, r16],
                           v16.at[0, r16], v16h.at[0, r16], 0))
            for j in range(n4 // BLOCK):
                keep(1, pl.ds(j * 4 * BLOCK + r, BLOCK, stride=4),
                     kept(2, slice(r * n4 + j * BLOCK, r * n4 + (j + 1) * BLOCK)))
        for r in range(4):
            for j in range(B_NSUB // 4):
                keep(0, pl.ds(j * 4 * BLOCK + r, BLOCK, stride=4),
                     block(1, q4[0, r, j * BLOCK:(j + 1) * BLOCK, lanes], k4.at[0, r],
                           k4h.at[0, r], v4.at[0, r], v4h.at[0, r], j))
        for j in range(B_NSUB):
            rows = slice(j * BLOCK, (j + 1) * BLOCK)
            parts = [block(0, q1[0, rows, lanes], k1.at[0], k1h.at[0], v1.at[0], v1h.at[0], j),
                     kept(0, rows), kept(1, rows)]
            mx = jnp.maximum(jnp.maximum(parts[0][2], parts[1][2]), parts[2][2])
            num = den = None
            for pv_n, l_n, m_n in parts:
                w = jnp.exp2(m_n - mx)
                num = w * pv_n if num is None else num + w * pv_n
                den = w * l_n if den is None else den + w * l_n
            o_ref[0, rows, lanes] = num / den


def _attention_b(qkv, bias, batch):
    b1, b4, b16 = qkv
    S = b1.shape[0] // batch
    assert S % B_TILE == 0
    b1 = b1.reshape(batch, S, 3 * Q_W)
    nstep = N_HEADS // (2 * B_PAIRS)
    W = B_PAIRS * LANES
    n4, n16 = B_TILE // 4, B_TILE // 16
    assert n16 == BLOCK

    def specs(part):
        col = lambda hg: part * nstep + hg
        nat = pl.BlockSpec((1, B_TILE, W), lambda bi, hg, ti: (bi, ti, col(hg)))
        nat_h = pl.BlockSpec((1, BLOCK, W), lambda bi, hg, ti: (
            bi, jnp.maximum(ti * B_NSUB - 1, 0), col(hg)))
        d4 = pl.BlockSpec((1, 4, n4, W), lambda bi, hg, ti: (bi, 0, ti, col(hg)))
        d4_h = pl.BlockSpec((1, 4, BLOCK, W), lambda bi, hg, ti: (
            bi, 0, jnp.maximum(ti * (n4 // BLOCK) - 1, 0), col(hg)))
        d16 = pl.BlockSpec((1, 16, n16, W), lambda bi, hg, ti: (bi, 0, ti, col(hg)))
        d16_h = pl.BlockSpec((1, 16, BLOCK, W), lambda bi, hg, ti: (
            bi, 0, jnp.maximum(ti - 1, 0), col(hg)))
        return nat, nat_h, d4, d4_h, d16, d16_h

    q_nat, _, q_d4, _, q_d16, _ = specs(0)
    kv_args = (b1, b1, b4, b4, b16, b16)
    state = pltpu.VMEM((3 * B_PAIRS, B_TILE, LANES), f32)
    return pl.pallas_call(
        _attn_b_kernel,
        grid=(batch, nstep, S // B_TILE),
        in_specs=[q_nat, q_d4, q_d16, *specs(1), *specs(2),
                  pl.BlockSpec((2, 3, 2 * B_PAIRS, BLOCK, 2 * BLOCK),
                               lambda bi, hg, ti: (0, 0, hg, 0, 0))],
        out_specs=pl.BlockSpec((1, B_TILE, W), lambda bi, hg, ti: (bi, ti, hg)),
        out_shape=jax.ShapeDtypeStruct((batch, S, Q_W), f32),
        scratch_shapes=[state, state, state],
        compiler_params=pltpu.CompilerParams(
            dimension_semantics=("parallel", "parallel", "arbitrary"),
            vmem_limit_bytes=VMEM_LIMIT),
        name="attn_b",
    )(b1, b4, b16, *kv_args, *kv_args, bias)


def _outproj_kernel(x_ref, ya_ref, ob_ref, g_ref, w_ref, o_ref, *, col_chunk):
    ob = ob_ref[...]
    ms = jnp.mean(ob * ob, axis=-1, keepdims=True)
    yb = ((ob * lax.rsqrt(ms + EPS)) * g_ref[...]).astype(bf16)
    ya = ya_ref[...]
    for c in range(0, o_ref.shape[1], col_chunk):
        cols = slice(c, c + col_chunk)
        acc = jnp.dot(ya, w_ref[0:Q_W, cols], preferred_element_type=f32)
        acc = acc + jnp.dot(yb, w_ref[Q_W:, cols], preferred_element_type=f32)
        o_ref[:, cols] = x_ref[:, cols] + acc


def _outproj(x2d, ya, ob, g_b, w_bf, *, tm=1024, col_chunk=512):
    T, D = x2d.shape
    tm = min(tm, T)
    assert T % tm == 0 and D % col_chunk == 0
    return pl.pallas_call(
        functools.partial(_outproj_kernel, col_chunk=col_chunk),
        grid=(T // tm,),
        in_specs=[
            pl.BlockSpec((tm, D), lambda i: (i, 0)),
            pl.BlockSpec((tm, Q_W), lambda i: (i, 0)),
            pl.BlockSpec((tm, Q_W), lambda i: (i, 0)),
            pl.BlockSpec((1, Q_W), lambda i: (0, 0)),
            pl.BlockSpec(w_bf.shape, lambda i: (0, 0), pipeline_mode=pl.Buffered(1)),
        ],
        out_specs=pl.BlockSpec((tm, D), lambda i: (i, 0)),
        out_shape=jax.ShapeDtypeStruct((T, D), f32),
        compiler_params=pltpu.CompilerParams(
            dimension_semantics=("parallel",), vmem_limit_bytes=VMEM_LIMIT),
        name="outproj",
    )(x2d, ya, ob, g_b, w_bf)


def _mlp_kernel(x_ref, g_ref, w1_ref, w2_ref, gf_ref, o_ref, h_scr, *, final_norm, row_chunk):
    f = pl.program_id(1)
    last = pl.num_programs(1) - 1
    tm = o_ref.shape[0]
    chunks = [slice(r, r + row_chunk) for r in range(0, tm, row_chunk)]

    def ff(h):
        u = jnp.maximum(jnp.dot(h, w1_ref[...], preferred_element_type=f32), 0.0)
        return jnp.dot((u * u).astype(bf16), w2_ref[...], preferred_element_type=f32)

    @pl.when(f == 0)
    def _():
        for rows in chunks:
            x = x_ref[rows, :]
            ms = jnp.mean(x * x, axis=-1, keepdims=True)
            h = ((x * lax.rsqrt(ms + EPS)) * g_ref[...]).astype(bf16)
            h_scr[rows, :] = h
            o_ref[rows, :] = x + ff(h)

    @pl.when(jnp.logical_and(f > 0, jnp.logical_or(f < last, not final_norm)))
    def _():
        o_ref[...] += ff(h_scr[...])

    if final_norm:
        @pl.when(jnp.logical_and(f > 0, f == last))
        def _():
            for rows in chunks:
                y = o_ref[rows, :] + ff(h_scr[rows, :])
                ms = jnp.mean(y * y, axis=-1, keepdims=True)
                o_ref[rows, :] = (y * lax.rsqrt(ms + EPS)) * gf_ref[...]


def _mlp(x2d, g, w1_bf, w2_bf, gf, *, final_norm, tm=1024, tf=1024, row_chunk=256):
    T, D = x2d.shape
    d_ff = w1_bf.shape[1]
    tm = min(tm, T)
    assert T % tm == 0 and d_ff % tf == 0 and d_ff // tf >= 2 and tm % row_chunk == 0
    return pl.pallas_call(
        functools.partial(_mlp_kernel, final_norm=final_norm, row_chunk=row_chunk),
        grid=(T // tm, d_ff // tf),
        in_specs=[
            pl.BlockSpec((tm, D), lambda i, f: (i, 0)),
            pl.BlockSpec((1, D), lambda i, f: (0, 0)),
            pl.BlockSpec((D, tf), lambda i, f: (0, f)),
            pl.BlockSpec((tf, D), lambda i, f: (f, 0)),
            pl.BlockSpec((1, D), lambda i, f: (0, 0)),
        ],
        out_specs=pl.BlockSpec((tm, D), lambda i, f: (i, 0)),
        out_shape=jax.ShapeDtypeStruct((T, D), f32),
        scratch_shapes=[pltpu.VMEM((tm, D), bf16)],
        compiler_params=pltpu.CompilerParams(
            dimension_semantics=("parallel", "arbitrary"), vmem_limit_bytes=VMEM_LIMIT),
        name="mlp",
    )(x2d, g, w1_bf, w2_bf, gf)


def kernel(x, g_attn, w_in, b_in, sinks_a, g_out_a, g_out_b, w_out, g_mlp, w_1, w_2, g_final):
    b, S, D = x.shape
    depth = w_in.shape[0]
    T = b * S
    bias_a = jnp.asarray(_band_bias(WINDOW_A - 1, 1))
    bias_b = jnp.asarray(np.stack([_band_bias(w // d, d) for w, d in DILATED_BRANCHES], axis=1))

    x2d = x.reshape(T, D)
    for l in range(depth):
        qa, kva, *qkv_b = _inproj(x2d, g_attn[l][None], w_in[l].astype(bf16), b_in[l][None], b)
        r3 = lambda t: t.reshape(b, S, t.shape[-1])
        ya = _attention_a(r3(qa), r3(kva), bias_a, sinks_a[l], g_out_a[l][None])
        ob = _attention_b(qkv_b, bias_b, b)
        x1 = _outproj(x2d, ya.reshape(T, Q_W), ob.reshape(T, Q_W), g_out_b[l][None],
                      w_out[l].astype(bf16))
        x2d = _mlp(x1, g_mlp[l][None], w_1[l].astype(bf16), w_2[l].astype(bf16), g_final[None],
                   final_norm=(l == depth - 1))
    return x2d.reshape(b, S, D)
```

```python
import functools

import numpy as np
import jax
import jax.numpy as jnp
from jax import lax
from jax.experimental import pallas as pl
from jax.experimental.pallas import tpu as pltpu

HEAD_DIM = 64
N_HEADS = 16
N_KV_A = 2
BLOCK = 128
WINDOW_A = 128
DILATED_BRANCHES = ((128, 1), (512, 4), (2048, 16))
EPS = 1e-5
NEG_INF = -1e30
LOG2E = 1.4426950408889634
Q_W = N_HEADS * HEAD_DIM
KV_A_W = N_KV_A * HEAD_DIM
LANES = 128
PACK = 16
VMEM_LIMIT = 60 * 1024 * 1024

bf16 = jnp.bfloat16
f32 = jnp.float32
_NT = (((1,), (1,)), ((), ()))


def _alibi_slopes(n):
    return 2.0 ** (-8.0 * (np.arange(n) + 1) / n)


def _band_bias(max_steps, step_dist):
    qi = np.arange(BLOCK)[:, None]
    kj = np.arange(2 * BLOCK)[None, :]
    steps = qi + BLOCK - kj
    valid = (steps >= 0) & (steps <= max_steps)
    slopes = _alibi_slopes(N_HEADS).astype(np.float32)
    pen = slopes[:, None, None] * (step_dist * steps).astype(np.float32)[None]
    pen = pen.astype(np.float64) * LOG2E
    gen = np.where(valid[None], -pen, np.float32(NEG_INF)).astype(np.float32)
    first = np.where((valid & (kj >= BLOCK))[None], -pen, np.float32(NEG_INF)).astype(np.float32)
    return np.stack([gen, first])


def _inproj_kernel(x_ref, g_ref, w_ref, b_ref, qa_ref, kva_ref, b1_ref, b4_ref, b16_ref,
                   ys, ys4, *, chunk, row_chunk):
    tm = x_ref.shape[0]
    qscale = HEAD_DIM ** -0.5 * LOG2E
    n4, n16 = row_chunk // 4, row_chunk // 16
    c0 = Q_W + 2 * KV_A_W
    slot = 0
    for part in range(tm // row_chunk):
        rows = slice(part * row_chunk, (part + 1) * row_chunk)
        x = x_ref[rows, :]
        ms = jnp.mean(x * x, axis=-1, keepdims=True)
        h = ((x * lax.rsqrt(ms + EPS)) * g_ref[...]).astype(bf16)

        def project(col, ch, scale):
            y = jnp.dot(h, w_ref[:, col:col + ch], preferred_element_type=f32)
            y = y + b_ref[:, col:col + ch]
            return y if scale is None else y * scale

        for cc in range(0, Q_W, chunk):
            qa_ref[rows, cc:cc + chunk] = project(cc, chunk, qscale).astype(bf16)
        kva_ref[rows, :] = project(Q_W, 2 * KV_A_W, None).astype(bf16)

        rows4 = slice(part * n4, (part + 1) * n4)
        rows16 = slice(part * n16, (part + 1) * n16)
        for cc in range(0, 3 * Q_W, chunk):
            y = project(c0 + cc, chunk, qscale if cc < Q_W else None)
            b1_ref[rows, cc:cc + chunk] = y.astype(bf16)
            for s in range(chunk // LANES):
                lanes = slice(cc + s * LANES, cc + (s + 1) * LANES)
                ys[slot, s] = y[:, s * LANES:(s + 1) * LANES]
                for r in range(4):
                    t = ys[slot, s, pl.ds(r, n4, stride=4), :]
                    b4_ref[0, r, rows4, lanes] = t.astype(bf16)
                    ys4[slot, s, r * n4:(r + 1) * n4, :] = t
                for r in range(4):
                    for c in range(4):
                        b16_ref[0, r + 4 * c, rows16, lanes] = ys4[
                            slot, s, pl.ds(r * n4 + c, n16, stride=4), :].astype(bf16)
            slot = 1 - slot


def _inproj(x2d, g, w_bf, b_in, batch, *, tm=512, chunk=512, row_chunk=512):
    T, D = x2d.shape
    S = T // batch
    d_in = w_bf.shape[1]
    assert 2 * KV_A_W + 4 * Q_W == d_in and S % tm == 0 and Q_W % chunk == 0
    tiles = S // tm
    const = lambda i: (0, 0)
    row = lambda i: (i, 0)
    res = lambda i: (i // tiles, 0, i % tiles, 0)
    return pl.pallas_call(
        functools.partial(_inproj_kernel, chunk=chunk, row_chunk=row_chunk),
        grid=(T // tm,),
        in_specs=[
            pl.BlockSpec((tm, D), row),
            pl.BlockSpec((1, D), const),
            pl.BlockSpec((D, d_in), const, pipeline_mode=pl.Buffered(1)),
            pl.BlockSpec((1, d_in), const),
        ],
        out_specs=[pl.BlockSpec((tm, Q_W), row),
                   pl.BlockSpec((tm, 2 * KV_A_W), row),
                   pl.BlockSpec((tm, 3 * Q_W), row),
                   pl.BlockSpec((1, 4, tm // 4, 3 * Q_W), res),
                   pl.BlockSpec((1, 16, tm // 16, 3 * Q_W), res)],
        out_shape=[jax.ShapeDtypeStruct((T, Q_W), bf16),
                   jax.ShapeDtypeStruct((T, 2 * KV_A_W), bf16),
                   jax.ShapeDtypeStruct((T, 3 * Q_W), bf16),
                   jax.ShapeDtypeStruct((batch, 4, S // 4, 3 * Q_W), bf16),
                   jax.ShapeDtypeStruct((batch, 16, S // 16, 3 * Q_W), bf16)],
        scratch_shapes=[pltpu.VMEM((2, chunk // LANES, row_chunk, LANES), f32),
                        pltpu.VMEM((2, chunk // LANES, row_chunk, LANES), f32)],
        compiler_params=pltpu.CompilerParams(
            dimension_semantics=("parallel",), vmem_limit_bytes=VMEM_LIMIT),
        name="inproj",
    )(x2d, g, w_bf, b_in)


def _attn_a_kernel(q_ref, k_ref, kp_ref, v_ref, vp_ref, bias_ref, sink_ref, g_ref, out_ref,
                   k_scr, v_scr, o_scr, *, tq):
    li = pl.program_id(1)
    lane = lax.broadcasted_iota(jnp.int32, (1, LANES), 1)
    lo = lane < HEAD_DIM

    def dup(t):
        r = pltpu.roll(t.astype(f32), HEAD_DIM, 1).astype(bf16)
        return jnp.concatenate([jnp.where(lo, t, r), jnp.where(lo, r, t)], axis=1)
    k_scr[0:BLOCK, :] = dup(kp_ref[0])
    k_scr[BLOCK:, :] = dup(k_ref[0])
    v_scr[0:BLOCK, :] = dup(vp_ref[0])
    v_scr[BLOCK:, :] = dup(v_ref[0])

    first = jnp.where(li == 0, 1, 0)
    ones = jnp.ones((2 * BLOCK, LANES), bf16)
    lane0 = lane == 0
    row0 = lax.broadcasted_iota(jnp.int32, (PACK, 1), 0) == 0
    assert WINDOW_A <= BLOCK
    for j in range(tq // BLOCK):
        rows = slice(j * BLOCK, (j + 1) * BLOCK)
        krows = slice(j * BLOCK, (j + 2) * BLOCK)
        sel = first if j == 0 else 0
        for hp in range(N_HEADS // 2):
            cols = slice(hp * LANES, (hp + 1) * LANES)
            g = (2 * hp) // (N_HEADS // N_KV_A)
            kcols = slice(g * LANES, (g + 1) * LANES)
            q2 = q_ref[0, rows, cols]
            k2 = k_scr[krows, kcols]
            v_top = jnp.where(row0, jnp.zeros((PACK, LANES), bf16),
                              v_scr[j * BLOCK:j * BLOCK + PACK, kcols])
            v2 = jnp.concatenate([v_top, v_scr[j * BLOCK + PACK:(j + 2) * BLOCK, kcols]], axis=0)
            v2a = jnp.concatenate([v2, ones], axis=1)
            zero = jnp.zeros_like(q2)
            pv, l = [], []
            for e in range(2):
                h = 2 * hp + e
                qm = jnp.where(lo, q2, zero) if e == 0 else jnp.where(lo, zero, q2)
                s = lax.dot_general(qm, k2, _NT, preferred_element_type=f32)
                s_prev = jnp.where(lane0, sink_ref[h] * LOG2E,
                                   s[:, :BLOCK] + bias_ref[sel, h, :, :BLOCK])
                s = jnp.concatenate([s_prev, s[:, BLOCK:] + bias_ref[sel, h, :, BLOCK:]], axis=1)
                p = jnp.exp2(s - jnp.max(s, axis=-1, keepdims=True))
                pva = jnp.dot(p.astype(bf16), v2a, preferred_element_type=f32)
                pv.append(pva[:, :LANES])
                l.append(pva[:, LANES:])
            o_scr[rows, cols] = jnp.where(lo, pv[0], pv[1]) / jnp.where(lo, l[0], l[1])

    o = o_scr[...]
    ms = jnp.mean(o * o, axis=-1, keepdims=True)
    out_ref[0] = ((o * lax.rsqrt(ms + EPS)) * g_ref[...]).astype(bf16)


def _attention_a(q, kv, bias, sinks, gain, *, tq=2048):
    b, S, _ = q.shape
    tq = min(tq, S)
    assert S % tq == 0 and tq % BLOCK == 0
    nsub = tq // BLOCK
    cur = lambda bi, li: (bi, li, 0)
    k, v = kv, kv
    return pl.pallas_call(
        functools.partial(_attn_a_kernel, tq=tq),
        grid=(b, S // tq),
        in_specs=[
            pl.BlockSpec((1, tq, Q_W), cur),
            pl.BlockSpec((1, tq, KV_A_W), lambda bi, li: (bi, li, 0)),
            pl.BlockSpec((1, BLOCK, KV_A_W), lambda bi, li: (bi, jnp.maximum(li * nsub - 1, 0), 0)),
            pl.BlockSpec((1, tq, KV_A_W), lambda bi, li: (bi, li, 1)),
            pl.BlockSpec((1, BLOCK, KV_A_W), lambda bi, li: (bi, jnp.maximum(li * nsub - 1, 0), 1)),
            pl.BlockSpec(bias.shape, lambda bi, li: (0, 0, 0, 0)),
            pl.BlockSpec(memory_space=pltpu.SMEM),
            pl.BlockSpec((1, Q_W), lambda bi, li: (0, 0)),
        ],
        out_specs=pl.BlockSpec((1, tq, Q_W), cur),
        out_shape=jax.ShapeDtypeStruct((b, S, Q_W), bf16),
        scratch_shapes=[pltpu.VMEM((tq + BLOCK, 2 * KV_A_W), bf16),
                        pltpu.VMEM((tq + BLOCK, 2 * KV_A_W), bf16),
                        pltpu.VMEM((tq, Q_W), f32)],
        compiler_params=pltpu.CompilerParams(
            dimension_semantics=("parallel", "arbitrary"), vmem_limit_bytes=VMEM_LIMIT),
        name="attn_a",
    )(q, k, k, v, v, bias, sinks, gain)


B_TILE = 2048
B_DILS = tuple(d for _, d in DILATED_BRANCHES)
assert B_DILS == (1, 4, 16) and all(w // d == BLOCK for w, d in DILATED_BRANCHES)
B_NSUB = B_TILE // BLOCK
B_PAIRS = 2


def _attn_b_kernel(q1, q4, q16, k1, k1h, k4, k4h, k16, k16h, v1, v1h, v4, v4h, v16, v16h,
                   bias_ref, o_ref, pv_st, l_st, m_st):
    first = jnp.where(pl.program_id(2) == 0, 1, 0)
    lane = lax.broadcasted_iota(jnp.int32, (1, LANES), 1)
    lo = lane < HEAD_DIM
    ones = jnp.ones((2 * BLOCK, LANES), bf16)

    def pair_block(br, h0, q2, k2, v2, sel):
        v2a = jnp.concatenate([v2, ones], axis=1)
        zero = jnp.zeros_like(q2)
        pv, l, m = [], [], []
        for e in range(2):
            qm = jnp.where(lo, q2, zero) if e == 0 else jnp.where(lo, zero, q2)
            s = lax.dot_general(qm, k2, _NT, preferred_element_type=f32)
            s = s + bias_ref[sel, br, h0 + e]
            me = jnp.max(s, axis=-1, keepdims=True)
            p = jnp.exp2(s - me)
            pva = jnp.dot(p.astype(bf16), v2a, preferred_element_type=f32)
            pv.append(pva[:, :LANES])
            l.append(pva[:, LANES:])
            m.append(jnp.broadcast_to(me, (BLOCK, LANES)))
        return (jnp.where(lo, pv[0], pv[1]), jnp.where(lo, l[0], l[1]),
                jnp.where(lo, m[0], m[1]))

    def window(cur, halo, j, lanes):
        if j == 0:
            return jnp.concatenate([halo[:, lanes], cur[0:BLOCK, lanes]], axis=0)
        return cur[(j - 1) * BLOCK:(j + 1) * BLOCK, lanes]

    n4 = B_TILE // 4
    for pair in range(B_PAIRS):
        lanes = slice(pair * LANES, (pair + 1) * LANES)

        def keep(slot, rows, res):
            s = slot * B_PAIRS + pair
            pv_st[s, rows, :], l_st[s, rows, :], m_st[s, rows, :] = res

        def kept(slot, rows):
            s = slot * B_PAIRS + pair
            return pv_st[s, rows, :], l_st[s, rows, :], m_st[s, rows, :]

        def block(br, q2, k_cur, k_halo, v_cur, v_halo, j):
            return pair_block(br, 2 * pair, q2, window(k_cur, k_halo, j, lanes),
                              window(v_cur, v_halo, j, lanes), first if j == 0 else 0)

        for r in range(4):
            for c in range(4):
                r16 = r + 4 * c
                keep(2, pl.ds(r * n4 + c, BLOCK, stride=4),
                     block(2, q16[0, r16, :, lanes], k16.at[0, r16], k16h.at[0, r16],
                           v16.at[0, r16], v16h.at[0, r16], 0))
            for j in range(n4 // BLOCK):
                keep(1, pl.ds(j * 4 * BLOCK + r, BLOCK, stride=4),
                     kept(2, slice(r * n4 + j * BLOCK, r * n4 + (j + 1) * BLOCK)))
        for r in range(4):
            for j in range(B_NSUB // 4):
                keep(0, pl.ds(j * 4 * BLOCK + r, BLOCK, stride=4),
                     block(1, q4[0, r, j * BLOCK:(j + 1) * BLOCK, lanes], k4.at[0, r],
                           k4h.at[0, r], v4.at[0, r], v4h.at[0, r], j))
        for j in range(B_NSUB):
            rows = slice(j * BLOCK, (j + 1) * BLOCK)
            parts = [block(0, q1[0, rows, lanes], k1.at[0], k1h.at[0], v1.at[0], v1h.at[0], j),
                     kept(0, rows), kept(1, rows)]
            mx = jnp.maximum(jnp.maximum(parts[0][2], parts[1][2]), parts[2][2])
            num = den = None
            for pv_n, l_n, m_n in parts:
                w = jnp.exp2(m_n - mx)
                num = w * pv_n if num is None else num + w * pv_n
                den = w * l_n if den is None else den + w * l_n
            o_ref[0, rows, lanes] = num / den


def _attention_b(qkv, bias, batch):
    b1, b4, b16 = qkv
    S = b1.shape[0] // batch
    assert S % B_TILE == 0
    b1 = b1.reshape(batch, S, 3 * Q_W)
    nstep = N_HEADS // (2 * B_PAIRS)
    W = B_PAIRS * LANES
    n4, n16 = B_TILE // 4, B_TILE // 16
    assert n16 == BLOCK

    def specs(part):
        col = lambda hg: part * nstep + hg
        nat = pl.BlockSpec((1, B_TILE, W), lambda bi, hg, ti: (bi, ti, col(hg)))
        nat_h = pl.BlockSpec((1, BLOCK, W), lambda bi, hg, ti: (
            bi, jnp.maximum(ti * B_NSUB - 1, 0), col(hg)))
        d4 = pl.BlockSpec((1, 4, n4, W), lambda bi, hg, ti: (bi, 0, ti, col(hg)))
        d4_h = pl.BlockSpec((1, 4, BLOCK, W), lambda bi, hg, ti: (
            bi, 0, jnp.maximum(ti * (n4 // BLOCK) - 1, 0), col(hg)))
        d16 = pl.BlockSpec((1, 16, n16, W), lambda bi, hg, ti: (bi, 0, ti, col(hg)))
        d16_h = pl.BlockSpec((1, 16, BLOCK, W), lambda bi, hg, ti: (
            bi, 0, jnp.maximum(ti - 1, 0), col(hg)))
        return nat, nat_h, d4, d4_h, d16, d16_h

    q_nat, _, q_d4, _, q_d16, _ = specs(0)
    kv_args = (b1, b1, b4, b4, b16, b16)
    state = pltpu.VMEM((3 * B_PAIRS, B_TILE, LANES), f32)
    return pl.pallas_call(
        _attn_b_kernel,
        grid=(batch, nstep, S // B_TILE),
        in_specs=[q_nat, q_d4, q_d16, *specs(1), *specs(2),
                  pl.BlockSpec((2, 3, 2 * B_PAIRS, BLOCK, 2 * BLOCK),
                               lambda bi, hg, ti: (0, 0, hg, 0, 0))],
        out_specs=pl.BlockSpec((1, B_TILE, W), lambda bi, hg, ti: (bi, ti, hg)),
        out_shape=jax.ShapeDtypeStruct((batch, S, Q_W), f32),
        scratch_shapes=[state, state, state],
        compiler_params=pltpu.CompilerParams(
            dimension_semantics=("parallel", "parallel", "arbitrary"),
            vmem_limit_bytes=VMEM_LIMIT),
        name="attn_b",
    )(b1, b4, b16, *kv_args, *kv_args, bias)


def _outproj_kernel(x_ref, ya_ref, ob_ref, g_ref, w_ref, o_ref, *, col_chunk):
    ob = ob_ref[...]
    ms = jnp.mean(ob * ob, axis=-1, keepdims=True)
    yb = ((ob * lax.rsqrt(ms + EPS)) * g_ref[...]).astype(bf16)
    ya = ya_ref[...]
    for c in range(0, o_ref.shape[1], col_chunk):
        cols = slice(c, c + col_chunk)
        acc = jnp.dot(ya, w_ref[0:Q_W, cols], preferred_element_type=f32)
        acc = acc + jnp.dot(yb, w_ref[Q_W:, cols], preferred_element_type=f32)
        o_ref[:, cols] = x_ref[:, cols] + acc


def _outproj(x2d, ya, ob, g_b, w_bf, *, tm=1024, col_chunk=512):
    T, D = x2d.shape
    tm = min(tm, T)
    assert T % tm == 0 and D % col_chunk == 0
    return pl.pallas_call(
        functools.partial(_outproj_kernel, col_chunk=col_chunk),
        grid=(T // tm,),
        in_specs=[
            pl.BlockSpec((tm, D), lambda i: (i, 0)),
            pl.BlockSpec((tm, Q_W), lambda i: (i, 0)),
            pl.BlockSpec((tm, Q_W), lambda i: (i, 0)),
            pl.BlockSpec((1, Q_W), lambda i: (0, 0)),
            pl.BlockSpec(w_bf.shape, lambda i: (0, 0), pipeline_mode=pl.Buffered(1)),
        ],
        out_specs=pl.BlockSpec((tm, D), lambda i: (i, 0)),
        out_shape=jax.ShapeDtypeStruct((T, D), f32),
        compiler_params=pltpu.CompilerParams(
            dimension_semantics=("parallel",), vmem_limit_bytes=VMEM_LIMIT),
        name="outproj",
    )(x2d, ya, ob, g_b, w_bf)


def _mlp_kernel(x_ref, g_ref, w1_ref, w2_ref, gf_ref, o_ref, h_scr, *, final_norm, row_chunk):
    f = pl.program_id(1)
    last = pl.num_programs(1) - 1
    tm = o_ref.shape[0]
    chunks = [slice(r, r + row_chunk) for r in range(0, tm, row_chunk)]

    def ff(h):
        u = jnp.maximum(jnp.dot(h, w1_ref[...], preferred_element_type=f32), 0.0)
        return jnp.dot((u * u).astype(bf16), w2_ref[...], preferred_element_type=f32)

    @pl.when(f == 0)
    def _():
        for rows in chunks:
            x = x_ref[rows, :]
            ms = jnp.mean(x * x, axis=-1, keepdims=True)
            h = ((x * lax.rsqrt(ms + EPS)) * g_ref[...]).astype(bf16)
            h_scr[rows, :] = h
            o_ref[rows, :] = x + ff(h)

    @pl.when(jnp.logical_and(f > 0, jnp.logical_or(f < last, not final_norm)))
    def _():
        o_ref[...] += ff(h_scr[...])

    if final_norm:
        @pl.when(jnp.logical_and(f > 0, f == last))
        def _():
            for rows in chunks:
                y = o_ref[rows, :] + ff(h_scr[rows, :])
                ms = jnp.mean(y * y, axis=-1, keepdims=True)
                o_ref[rows, :] = (y * lax.rsqrt(ms + EPS)) * gf_ref[...]


def _mlp(x2d, g, w1_bf, w2_bf, gf, *, final_norm, tm=1024, tf=1024, row_chunk=256):
    T, D = x2d.shape
    d_ff = w1_bf.shape[1]
    tm = min(tm, T)
    assert T % tm == 0 and d_ff % tf == 0 and d_ff // tf >= 2 and tm % row_chunk == 0
    return pl.pallas_call(
        functools.partial(_mlp_kernel, final_norm=final_norm, row_chunk=row_chunk),
        grid=(T // tm, d_ff // tf),
        in_specs=[
            pl.BlockSpec((tm, D), lambda i, f: (i, 0)),
            pl.BlockSpec((1, D), lambda i, f: (0, 0)),
            pl.BlockSpec((D, tf), lambda i, f: (0, f)),
            pl.BlockSpec((tf, D), lambda i, f: (f, 0)),
            pl.BlockSpec((1, D), lambda i, f: (0, 0)),
        ],
        out_specs=pl.BlockSpec((tm, D), lambda i, f: (i, 0)),
        out_shape=jax.ShapeDtypeStruct((T, D), f32),
        scratch_shapes=[pltpu.VMEM((tm, D), bf16)],
        compiler_params=pltpu.CompilerParams(
            dimension_semantics=("parallel", "arbitrary"), vmem_limit_bytes=VMEM_LIMIT),
        name="mlp",
    )(x2d, g, w1_bf, w2_bf, gf)


def kernel(x, g_attn, w_in, b_in, sinks_a, g_out_a, g_out_b, w_out, g_mlp, w_1, w_2, g_final):
    b, S, D = x.shape
    depth = w_in.shape[0]
    T = b * S
    bias_a = jnp.asarray(_band_bias(WINDOW_A - 1, 1))
    bias_b = jnp.asarray(np.stack([_band_bias(w // d, d) for w, d in DILATED_BRANCHES], axis=1))

    x2d = x.reshape(T, D)
    for l in range(depth):
        qa, kva, *qkv_b = _inproj(x2d, g_attn[l][None], w_in[l].astype(bf16), b_in[l][None], b)
        r3 = lambda t: t.reshape(b, S, t.shape[-1])
        ya = _attention_a(r3(qa), r3(kva), bias_a, sinks_a[l], g_out_a[l][None])
        ob = _attention_b(qkv_b, bias_b, b)
        x1 = _outproj(x2d, ya.reshape(T, Q_W), ob.reshape(T, Q_W), g_out_b[l][None],
                      w_out[l].astype(bf16))
        x2d = _mlp(x1, g_mlp[l][None], w_1[l].astype(bf16), w_2[l].astype(bf16), g_final[None],
                   final_norm=(l == depth - 1))
    return x2d.reshape(b, S, D)
```
